```python
import math
import jax, jax.numpy as jnp
from jax import lax
import numpy as np

D_MODEL = 1024
BATCH = 16
SEQ = 2048
DEPTH = 4

GLA_HEADS = 4
GLA_DK = 64
GLA_DV = 128
GLA_RANK = 16
GLA_GATE_TEMP = 16.0
GLA_CHUNK = 64
DIL_HEADS = 8
DIL_DH = 64
DIL_PATTERNS = ((128, 1), (512, 4), (2048, 16))
DIL_BLOCK = 128
MLSTM_HEADS = 4
MLSTM_DH = 128
MLSTM_CHUNK = 64
MLSTM_CONV = 4
D_FF = 2816
N_BRANCH = 3
DN_ALPHA = (2 * DEPTH) ** 0.25
DN_BETA = (8 * DEPTH) ** -0.25
LN_EPS = 1e-5

GLA_QK_W = GLA_HEADS * GLA_DK
GLA_V_W = GLA_HEADS * GLA_DV
DIL_W = DIL_HEADS * DIL_DH
ML_W = MLSTM_HEADS * MLSTM_DH
IN_WIDTHS = (GLA_QK_W, GLA_QK_W, GLA_V_W, GLA_V_W, GLA_RANK,
             DIL_W, DIL_W, DIL_W,
             ML_W, ML_W, ML_W, MLSTM_HEADS, MLSTM_HEADS, ML_W,
             N_BRANCH * D_MODEL)
D_IN = sum(IN_WIDTHS)

kernel_name = "hybrid_gla_dilated_mlstm_macaron_deepnorm"


def layer_norm(x, g, b):
    xf = x.astype(jnp.float32)
    mu = jnp.mean(xf, -1, keepdims=True)
    var = jnp.mean(jnp.square(xf - mu), -1, keepdims=True)
    return ((xf - mu) * lax.rsqrt(var + LN_EPS) * g + b).astype(x.dtype)


def head_group_norm(h, g):
    mu = jnp.mean(h, -1, keepdims=True)
    var = jnp.mean(jnp.square(h - mu), -1, keepdims=True)
    hn = (h - mu) * lax.rsqrt(var + LN_EPS)
    return hn.reshape(h.shape[0], h.shape[1], -1) * g.astype(jnp.float32)


def swiglu(x, w_gate, w_up, w_down):
    return (jax.nn.silu(x @ w_gate) * (x @ w_up)) @ w_down


def causal_depthwise_conv(x, w, b):
    K, C = w.shape
    y = lax.conv_general_dilated(x, w[:, None, :], window_strides=(1,), padding=[(K - 1, 0)],
                                 dimension_numbers=('NWC', 'WIO', 'NWC'), feature_group_count=C)
    return y + b


def gla_mixer(q, k, v, r, a_low, w_a2, b_a, g_norm):
    B_, S_, _ = q.shape
    L = GLA_CHUNK
    nc = S_ // L
    f32 = jnp.float32
    log_a = jax.nn.log_sigmoid((a_low @ w_a2 + b_a).astype(f32)) / GLA_GATE_TEMP

    def chunks(t, d):
        return t.astype(f32).reshape(B_, nc, L, GLA_HEADS, d).transpose(0, 3, 1, 2, 4)

    qc = chunks(q, GLA_DK) * GLA_DK ** -0.5
    kc = chunks(k, GLA_DK)
    vc = chunks(v, GLA_DV)
    bc = jnp.cumsum(chunks(log_a, GLA_DK), axis=3)
    b_last = bc[..., -1:, :]
    q_dec = qc * jnp.exp(bc)
    k_inv = kc * jnp.exp(-bc)
    causal = jnp.tril(jnp.ones((L, L), bool))
    att = jnp.where(causal, jnp.einsum('bhctk,bhcsk->bhcts', q_dec, k_inv), 0.0)
    o_intra = jnp.einsum('bhcts,bhcsv->bhctv', att, vc)
    kv_chunk = jnp.einsum('bhcsk,bhcsv->bhckv', kc * jnp.exp(b_last - bc), vc)
    decay = jnp.exp(b_last[..., 0, :])

    def step(state, inp):
        dec, kv = inp
        return dec[..., None] * state + kv, state

    s0 = jnp.zeros((B_, GLA_HEADS, GLA_DK, GLA_DV), f32)
    _, s_prev = lax.scan(step, s0, (jnp.moveaxis(decay, 2, 0), jnp.moveaxis(kv_chunk, 2, 0)))
    s_prev = jnp.moveaxis(s_prev, 0, 2)
    o = o_intra + jnp.einsum('bhctk,bhckv->bhctv', q_dec, s_prev)
    o = o.transpose(0, 2, 3, 1, 4).reshape(B_, S_, GLA_HEADS, GLA_DV)
    y = head_group_norm(o, g_norm) * jax.nn.silu(r.astype(f32))
    return y.astype(q.dtype)


def _strided_window_attention(qh, kh, vh, n_back, dil, slopes):
    B_, S_, H, dh = qh.shape
    f32 = jnp.float32
    BLK = DIL_BLOCK
    Ls = S_ // dil
    nb = -(-Ls // BLK)
    Lp = nb * BLK

    def sub(t):
        t = t.astype(f32).reshape(B_, Ls, dil, H, dh).transpose(0, 2, 3, 1, 4)
        t = jnp.pad(t, ((0, 0), (0, 0), (0, 0), (0, Lp - Ls), (0, 0)))
        return t.reshape(B_, dil, H, nb, BLK, dh)

    def with_prev(t):
        prev = jnp.pad(t, ((0, 0), (0, 0), (0, 0), (1, 0), (0, 0), (0, 0)))[:, :, :, :-1]
        return jnp.concatenate([prev, t], axis=4)

    qb = sub(qh) * dh ** -0.5
    kcat = with_prev(sub(kh))
    vcat = with_prev(sub(vh))
    qpos = jnp.arange(nb)[:, None, None] * BLK + jnp.arange(BLK)[None, :, None]
    kpos = jnp.arange(nb)[:, None, None] * BLK + jnp.arange(2 * BLK)[None, None, :] - BLK
    dist = qpos - kpos
    valid = (dist >= 0) & (dist <= n_back) & (kpos >= 0)
    bias = -slopes[:, None, None, None] * (dist * dil).astype(f32)[None]
    s = jnp.einsum('bdhnqe,bdhnke->bdhnqk', qb, kcat) + bias
    s = jnp.where(valid, s, -jnp.inf)
    lse = jax.nn.logsumexp(s, axis=-1)
    p = jnp.exp(s - lse[..., None])
    o = jnp.einsum('bdhnqk,bdhnke->bdhnqe', p, vcat)
    o = o.reshape(B_, dil, H, Lp, dh)[:, :, :, :Ls].transpose(0, 3, 1, 2, 4).reshape(B_, S_, H, dh)
    lse = lse.reshape(B_, dil, H, Lp)[..., :Ls].transpose(0, 3, 1, 2).reshape(B_, S_, H)
    return o, lse


def dilated_attention(q, k, v):
    B_, S_, _ = q.shape
    shp = (B_, S_, DIL_HEADS, DIL_DH)
    qh, kh, vh = q.reshape(shp), k.reshape(shp), v.reshape(shp)
    slopes = 2.0 ** (-8.0 * jnp.arange(1, DIL_HEADS + 1, dtype=jnp.float32) / DIL_HEADS)
    outs, lses = [], []
    for window, dil in DIL_PATTERNS:
        o, lse = _strided_window_attention(qh, kh, vh, window // dil, dil, slopes)
        outs.append(o)
        lses.append(lse)
    w = jax.nn.softmax(jnp.stack(lses, 0), axis=0)
    out = jnp.sum(w[..., None] * jnp.stack(outs, 0), axis=0)
    return out.reshape(B_, S_, DIL_W).astype(q.dtype)


def mlstm_mixer(q, k, v, i_pre, f_pre, o_pre, g_norm):
    B_, S_, _ = q.shape
    L = MLSTM_CHUNK
    nc = S_ // L
    H, dh = MLSTM_HEADS, MLSTM_DH
    f32 = jnp.float32

    def chunks(t):
        return t.astype(f32).reshape(B_, nc, L, H, dh).transpose(0, 3, 1, 2, 4)

    def gchunks(t):
        return t.astype(f32).reshape(B_, nc, L, H).transpose(0, 3, 1, 2)

    qc, kc, vc = chunks(q), chunks(k) * dh ** -0.5, chunks(v)
    log_i = gchunks(i_pre)
    F = jnp.cumsum(jax.nn.log_sigmoid(gchunks(f_pre)), axis=-1)
    F_last = F[..., -1]
    causal = jnp.tril(jnp.ones((L, L), bool))
    Dlog = jnp.where(causal, F[..., :, None] - F[..., None, :] + log_i[..., None, :], -jnp.inf)
    m_intra = jnp.max(Dlog, axis=-1)
    P = jnp.exp(Dlog - m_intra[..., None])
    Sqk = jnp.einsum('bhctd,bhcsd->bhcts', qc, kc) * P
    num_intra = jnp.einsum('bhcts,bhcsd->bhctd', Sqk, vc)
    den_intra = jnp.sum(Sqk, axis=-1)
    g = F_last[..., None] - F + log_i
    m_chunk = jnp.max(g, axis=-1)
    wgt = jnp.exp(g - m_chunk[..., None])
    kv_chunk = jnp.einsum('bhcs,bhcsv,bhcsk->bhcvk', wgt, vc, kc)
    n_chunk = jnp.einsum('bhcs,bhcsk->bhck', wgt, kc)

    def step(carry, inp):
        C, n, m = carry
        fl, mc, kv, nk = inp
        m_new = jnp.maximum(fl + m, mc)
        a = jnp.exp(fl + m - m_new)
        b = jnp.exp(mc - m_new)
        return (a[..., None, None] * C + b[..., None, None] * kv, a[..., None] * n + b[..., None] * nk, m_new), (C, n, m)

    init = (jnp.zeros((B_, H, dh, dh), f32), jnp.zeros((B_, H, dh), f32), jnp.zeros((B_, H), f32))
    xs = (jnp.moveaxis(F_last, 2, 0), jnp.moveaxis(m_chunk, 2, 0),
          jnp.moveaxis(kv_chunk, 2, 0), jnp.moveaxis(n_chunk, 2, 0))
    _, (C_prev, n_prev, m_prev) = lax.scan(step, init, xs)
    C_prev = jnp.moveaxis(C_prev, 0, 2)
    n_prev = jnp.moveaxis(n_prev, 0, 2)
    m_prev = jnp.moveaxis(m_prev, 0, 2)
    a_t = F + m_prev[..., None]
    m_t = jnp.maximum(a_t, m_intra)
    w_inter = jnp.exp(a_t - m_t)
    w_intra = jnp.exp(m_intra - m_t)
    num = w_inter[..., None] * jnp.einsum('bhcvk,bhctk->bhctv', C_prev, qc) + w_intra[..., None] * num_intra
    den = w_inter * jnp.einsum('bhck,bhctk->bhct', n_prev, qc) + w_intra * den_intra
    h = num / jnp.maximum(jnp.abs(den), jnp.exp(-m_t))[..., None]
    h = h.transpose(0, 2, 3, 1, 4).reshape(B_, S_, H, dh)
    y = head_group_norm(h, g_norm) * jax.nn.sigmoid(o_pre.astype(f32))
    return y.astype(q.dtype)


def hybrid_mixer(x, w_in, gla_w_a2, gla_b_a, gla_norm_g, ml_conv_w, ml_conv_b, ml_b_i, ml_b_f,
                 ml_norm_g, w_proj_gla, w_proj_dil, w_proj_ml, b_gate, w_out):
    B_, S_, D = x.shape
    z = x @ w_in
    split_points = np.cumsum(IN_WIDTHS)[:-1].tolist()
    (gq, gk, gv, gr, ga, dq, dk, dv, mq, mk, mv, mi, mf, mo, gates) = jnp.split(z, split_points, axis=-1)
    y_gla = gla_mixer(gq, gk, gv, gr, ga, gla_w_a2, gla_b_a, gla_norm_g)
    y_dil = dilated_attention(dq, dk, dv)
    mqk = jax.nn.silu(causal_depthwise_conv(jnp.concatenate([mq, mk], -1), ml_conv_w, ml_conv_b))
    mq, mk = jnp.split(mqk, 2, axis=-1)
    y_ml = mlstm_mixer(mq, mk, mv, mi + ml_b_i, mf + ml_b_f, mo, ml_norm_g)
    g = jax.nn.sigmoid(gates + b_gate).reshape(B_, S_, N_BRANCH, D)
    merged = (g[..., 0, :] * (y_gla @ w_proj_gla) + g[..., 1, :] * (y_dil @ w_proj_dil)
              + g[..., 2, :] * (y_ml @ w_proj_ml))
    return merged @ w_out


def setup_inputs(seed: int = 0) -> dict:
    key = jax.random.key(seed)
    ks = iter(jax.random.split(key, 32))

    def nrm(shape, scale):
        return jax.random.normal(next(ks), shape, jnp.float32) * scale

    L, D = DEPTH, D_MODEL
    return {
        "x": nrm((BATCH, SEQ, D), 1.0),
        "ffn1_w_gate": nrm((L, D, D_FF), D ** -0.5),
        "ffn1_w_up": nrm((L, D, D_FF), D ** -0.5),
        "ffn1_w_down": nrm((L, D_FF, D), D_FF ** -0.5 * DN_BETA),
        "ln1_g": 1.0 + nrm((L, D), 0.02),
        "ln1_b": nrm((L, D), 0.02),
        "w_in": nrm((L, D, D_IN), D ** -0.5),
        "gla_w_a2": nrm((L, GLA_RANK, GLA_QK_W), GLA_RANK ** -0.5),
        "gla_b_a": nrm((L, GLA_QK_W), 0.1),
        "gla_norm_g": 1.0 + nrm((L, GLA_V_W), 0.02),
        "ml_conv_w": nrm((L, MLSTM_CONV, 2 * ML_W), MLSTM_CONV ** -0.5),
        "ml_conv_b": nrm((L, 2 * ML_W), 0.02),
        "ml_b_i": nrm((L, MLSTM_HEADS), 0.1),
        "ml_b_f": 3.0 + nrm((L, MLSTM_HEADS), 0.5),
        "ml_norm_g": 1.0 + nrm((L, ML_W), 0.02),
        "w_proj_gla": nrm((L, GLA_V_W, D), GLA_V_W ** -0.5),
        "w_proj_dil": nrm((L, DIL_W, D), DIL_W ** -0.5),
        "w_proj_ml": nrm((L, ML_W, D), ML_W ** -0.5),
        "b_gate": nrm((L, N_BRANCH * D), 0.02),
        "w_out": nrm((L, D, D), D ** -0.5 * DN_BETA),
        "ln2_g": 1.0 + nrm((L, D), 0.02),
        "ln2_b": nrm((L, D), 0.02),
        "ffn2_w_gate": nrm((L, D, D_FF), D ** -0.5),
        "ffn2_w_up": nrm((L, D, D_FF), D ** -0.5),
        "ffn2_w_down": nrm((L, D_FF, D), D_FF ** -0.5 * DN_BETA),
        "ln3_g": 1.0 + nrm((L, D), 0.02),
        "ln3_b": nrm((L, D), 0.02),
    }


def reference(x, ffn1_w_gate, ffn1_w_up, ffn1_w_down, ln1_g, ln1_b, w_in, gla_w_a2, gla_b_a,
              gla_norm_g, ml_conv_w, ml_conv_b, ml_b_i, ml_b_f, ml_norm_g, w_proj_gla, w_proj_dil,
              w_proj_ml, b_gate, w_out, ln2_g, ln2_b, ffn2_w_gate, ffn2_w_up, ffn2_w_down, ln3_g, ln3_b):
    for l in range(DEPTH):
        x = layer_norm(DN_ALPHA * x + 0.5 * swiglu(x, ffn1_w_gate[l], ffn1_w_up[l], ffn1_w_down[l]),
                       ln1_g[l], ln1_b[l])
        mix = hybrid_mixer(x, w_in[l], gla_w_a2[l], gla_b_a[l], gla_norm_g[l], ml_conv_w[l], ml_conv_b[l],
                           ml_b_i[l], ml_b_f[l], ml_norm_g[l], w_proj_gla[l], w_proj_dil[l], w_proj_ml[l],
                           b_gate[l], w_out[l])
        x = layer_norm(DN_ALPHA * x + mix, ln2_g[l], ln2_b[l])
        x = layer_norm(DN_ALPHA * x + 0.5 * swiglu(x, ffn2_w_gate[l], ffn2_w_up[l], ffn2_w_down[l]),
                       ln3_g[l], ln3_b[l])
    return x
```

```python
import functools
import math

import jax
import jax.numpy as jnp
import numpy as np
from jax import lax
from jax.experimental import pallas as pl
from jax.experimental.pallas import tpu as pltpu

F32 = jnp.float32
BF16 = jnp.bfloat16

D_MODEL = 1024
DEPTH = 4
GLA_HEADS, GLA_DK, GLA_DV, GLA_RANK, GLA_CHUNK = 4, 64, 128, 16, 64
GLA_GATE_TEMP = 16.0
DIL_HEADS, DIL_DH, DIL_BLOCK = 8, 64, 128
DIL_PATTERNS = ((128, 1), (512, 4), (2048, 16))
ML_HEADS, ML_DH, ML_CHUNK, ML_CONV = 4, 128, 64, 4
D_FF = 2816
DN_ALPHA = (2 * DEPTH) ** 0.25
LN_EPS = 1e-5

GLA_QK_W = GLA_HEADS * GLA_DK
GLA_V_W = GLA_HEADS * GLA_DV
DIL_W = DIL_HEADS * DIL_DH
ML_W = ML_HEADS * ML_DH
IN_WIDTHS = (GLA_QK_W, GLA_QK_W, GLA_V_W, GLA_V_W, GLA_RANK, DIL_W, DIL_W, DIL_W,
             ML_W, ML_W, ML_W, ML_HEADS, ML_HEADS, ML_W, 3 * D_MODEL)
IN_OFFS = tuple(int(v) for v in np.cumsum((0,) + IN_WIDTHS))

LANES = 128
VMEM_LIMIT = 56 * 1024 * 1024
TM = 512
MIX_ROWS = 256

Z_GQ, Z_GK, Z_GV, Z_GR = 0, 256, 512, 1024
Z_DQ, Z_DK, Z_DV = 1536, 2048, 2560
Z_MQK, Z_MV, Z_MO = 3072, 4096, 4608
Z_SMALL = 5120
Z_W = 5248
SM_A, SM_I, SM_F = 0, 16, 20


def _ln(r, g, b):
    mu = jnp.mean(r, axis=-1, keepdims=True)
    d = r - mu
    var = jnp.mean(d * d, axis=-1, keepdims=True)
    return d * lax.rsqrt(var + LN_EPS) * g + b


def _log_sigmoid(x):
    return -(jnp.maximum(-x, 0.0) + jnp.log1p(jnp.exp(-jnp.abs(x))))


def _dot(a, b):
    return jnp.dot(a, b, preferred_element_type=F32)


def _dot_nt(a, b):
    return lax.dot_general(a, b, (((1,), (1,)), ((), ())), preferred_element_type=F32)


def _dot_tn(a, b):
    return lax.dot_general(a, b, (((0,), (0,)), ((), ())), preferred_element_type=F32)


def _dot_exact(a, b):
    return jnp.dot(a, b, preferred_element_type=F32, precision=lax.Precision.HIGHEST)


def _resident(shape, index_map):
    return pl.BlockSpec(shape, index_map, pipeline_mode=pl.Buffered(1))


def _params(*sem):
    return pltpu.CompilerParams(dimension_semantics=sem, vmem_limit_bytes=VMEM_LIMIT)


def _ffn_ln_kernel(x_ref, wg_ref, wu_ref, wd_ref, g_ref, b_ref, o_ref):
    x = x_ref[...]
    xb = x.astype(BF16)
    g = _dot(xb, wg_ref[...])
    u = _dot(xb, wu_ref[...])
    h = (g * jax.nn.sigmoid(g) * u).astype(BF16)
    y = _dot(h, wd_ref[...])
    o_ref[...] = _ln(DN_ALPHA * x + 0.5 * y, g_ref[...], b_ref[...])


def _ffn_ln(x2d, wg, wu, wd, g, b, l):
    m = x2d.shape[0]
    return pl.pallas_call(
        _ffn_ln_kernel,
        grid=(m // TM,),
        in_specs=[
            pl.BlockSpec((TM, D_MODEL), lambda i: (i, 0)),
            _resident((None, D_MODEL, D_FF), lambda i: (l, 0, 0)),
            _resident((None, D_MODEL, D_FF), lambda i: (l, 0, 0)),
            _resident((None, D_FF, D_MODEL), lambda i: (l, 0, 0)),
            _resident((None, 1, D_MODEL), lambda i: (l, 0, 0)),
            _resident((None, 1, D_MODEL), lambda i: (l, 0, 0)),
        ],
        out_specs=pl.BlockSpec((TM, D_MODEL), lambda i: (i, 0)),
        out_shape=jax.ShapeDtypeStruct((m, D_MODEL), F32),
        compiler_params=_params("parallel"),
        name="ffn_ln",
    )(x2d, wg, wu, wd, g, b)


def _in_proj_kernel(x_ref, w_ref, z_ref):
    z_ref[...] = _dot(x_ref[...].astype(BF16), w_ref[...])


def _in_proj(x2d, wz, l):
    m = x2d.shape[0]
    return pl.pallas_call(
        _in_proj_kernel,
        grid=(m // TM,),
        in_specs=[
            pl.BlockSpec((TM, D_MODEL), lambda i: (i, 0)),
            _resident((None, D_MODEL, Z_W), lambda i: (l, 0, 0)),
        ],
        out_specs=pl.BlockSpec((TM, Z_W), lambda i: (i, 0)),
        out_shape=jax.ShapeDtypeStruct((m, Z_W), F32),
        compiler_params=_params("parallel"),
        name="in_proj",
    )(x2d, wz)


def _chunk_masks(rows, chunk):
    r = lax.broadcasted_iota(jnp.int32, (rows, rows), 0)
    c = lax.broadcasted_iota(jnp.int32, (rows, rows), 1)
    same = (r // chunk) == (c // chunk)
    causal = same & (c <= r)
    return same, causal


def _gla_kernel(q_ref, k_ref, v_ref, r_ref, sm_ref, wa_ref, ba_ref, gn_ref, o_ref, st_ref):
    seq = q_ref.shape[0]
    rows, chunk = MIX_ROWS, GLA_CHUNK
    n_chunks = rows // chunk
    st_ref[...] = jnp.zeros(st_ref.shape, F32)

    same, causal = _chunk_masks(rows, chunk)
    tril = causal.astype(F32)
    ones = same.astype(F32)
    lane = lax.broadcasted_iota(jnp.int32, (1, LANES), 1)
    head_mask = (lane < GLA_DK, lane >= GLA_DK)

    def tile(i, carry):
        t0 = pl.multiple_of(i * rows, rows)
        sl = pl.ds(t0, rows)
        xa = _dot(sm_ref[sl, :].astype(BF16), wa_ref[...]) + ba_ref[...]
        la = _log_sigmoid(xa) * (1.0 / GLA_GATE_TEMP)
        bc = _dot_exact(tril, la)
        bl = _dot_exact(ones, la)
        q = q_ref[sl, :] * (GLA_DK ** -0.5)
        k = k_ref[sl, :]
        q_dec = (q * jnp.exp(bc))
        k_inv = (k * jnp.exp(-bc)).astype(BF16)
        k_dec = (k * jnp.exp(bl - bc)).astype(BF16)
        decay = jnp.exp(bl)
        for h in range(GLA_HEADS):
            pr = slice((h // 2) * LANES, (h // 2 + 1) * LANES)
            hs = slice(h * GLA_DV, (h + 1) * GLA_DV)
            qd = jnp.where(head_mask[h % 2], q_dec[:, pr], 0.0).astype(BF16)
            att = _dot_nt(qd, k_inv[:, pr])
            att = jnp.where(causal, att, 0.0).astype(BF16)
            vh = v_ref[sl, hs].astype(BF16)
            o_intra = _dot(att, vh)
            outs = []
            for c in range(n_chunks):
                cs = slice(c * chunk, (c + 1) * chunk)
                s_t = st_ref[h]
                outs.append(o_intra[cs] + _dot_nt(qd[cs], s_t.astype(BF16)))
                kv_t = _dot_tn(vh[cs], k_dec[cs, pr])
                st_ref[h] = s_t * decay[c * chunk:c * chunk + 1, pr] + kv_t
            o = jnp.concatenate(outs, axis=0)
            mu = jnp.mean(o, axis=-1, keepdims=True)
            d = o - mu
            var = jnp.mean(d * d, axis=-1, keepdims=True)
            hn = d * lax.rsqrt(var + LN_EPS)
            r = r_ref[sl, hs]
            y = hn * gn_ref[:, hs] * (r * jax.nn.sigmoid(r))
            o_ref[sl, hs] = y.astype(o_ref.dtype)
        return carry

    lax.fori_loop(0, seq // rows, tile, 0)


def _gla(z, wa, ba, gn, l):
    bsz, seq, _ = z.shape

    def zspec(width, off):
        return pl.BlockSpec((None, seq, width), lambda b: (b, 0, off // width))

    return pl.pallas_call(
        _gla_kernel,
        grid=(bsz,),
        in_specs=[
            zspec(GLA_QK_W, Z_GQ), zspec(GLA_QK_W, Z_GK), zspec(GLA_V_W, Z_GV), zspec(GLA_V_W, Z_GR),
            zspec(LANES, Z_SMALL),
            _resident((None, LANES, GLA_QK_W), lambda b: (l, 0, 0)),
            _resident((None, 1, GLA_QK_W), lambda b: (l, 0, 0)),
            _resident((None, 1, GLA_V_W), lambda b: (l, 0, 0)),
        ],
        out_specs=pl.BlockSpec((None, seq, GLA_V_W), lambda b: (b, 0, 0)),
        out_shape=jax.ShapeDtypeStruct((bsz, seq, GLA_V_W), BF16),
        scratch_shapes=[pltpu.VMEM((GLA_HEADS, GLA_DV, LANES), F32)],
        compiler_params=_params("parallel"),
        name="gla",
    )(z, z, z, z, z, wa, ba, gn)


def _dil_kernel(q_ref, k_ref, v_ref, bias_ref, o_ref, os_ref, ls_ref):
    seq = q_ref.shape[0]
    blk = DIL_BLOCK
    lane = lax.broadcasted_iota(jnp.int32, (1, LANES), 1)
    lo = lane < DIL_DH
    head_mask = (lo, jnp.logical_not(lo))
    scale = DIL_DH ** -0.5

    def attend(p, dil, start, pstart):
        qb = q_ref[pl.ds(start, blk, stride=dil), :] * scale
        kc = k_ref[pl.ds(start, blk, stride=dil), :]
        vc = v_ref[pl.ds(start, blk, stride=dil), :]
        if pstart is None:
            kcat, vcat = kc.astype(BF16), vc.astype(BF16)
        else:
            kp = k_ref[pl.ds(pstart, blk, stride=dil), :]
            vp = v_ref[pl.ds(pstart, blk, stride=dil), :]
            kcat = jnp.concatenate([kp, kc], axis=0).astype(BF16)
            vcat = jnp.concatenate([vp, vc], axis=0).astype(BF16)
        o_heads, lse_heads = [], []
        for e in range(2):
            qm = jnp.where(head_mask[e], qb, 0.0).astype(BF16)
            s = _dot_nt(qm, kcat)
            if pstart is None:
                s = s + bias_ref[p, e, :, blk:]
            else:
                s = s + bias_ref[p, e]
            m = jnp.max(s, axis=-1, keepdims=True)
            pe = jnp.exp(s - m)
            lsum = jnp.sum(pe, axis=-1, keepdims=True)
            o_heads.append(_dot(pe.astype(BF16), vcat) / lsum)
            lse_heads.append(m + jnp.log(lsum))
        os_ref[p, pl.ds(start, blk, stride=dil), :] = jnp.where(lo, o_heads[0], o_heads[1])
        ls_ref[p, pl.ds(start, blk, stride=dil), :] = jnp.where(lo, lse_heads[0], lse_heads[1])

    for p, (_, dil) in enumerate(DIL_PATTERNS):
        nb = seq // dil // blk
        span = dil * blk

        def first(r, carry, p=p, dil=dil):
            attend(p, dil, r, None)
            return carry

        lax.fori_loop(0, dil, first, 0)
        if nb > 1:
            def later(idx, carry, p=p, dil=dil, nb=nb, span=span):
                r = idx // (nb - 1)
                n = 1 + idx % (nb - 1)
                start = r + n * span
                attend(p, dil, start, start - span)
                return carry

            lax.fori_loop(0, dil * (nb - 1), later, 0)

    rows = MIX_ROWS

    def combine(i, carry):
        sl = pl.ds(pl.multiple_of(i * rows, rows), rows)
        l0, l1, l2 = ls_ref[0, sl, :], ls_ref[1, sl, :], ls_ref[2, sl, :]
        mx = jnp.maximum(jnp.maximum(l0, l1), l2)
        w0, w1, w2 = jnp.exp(l0 - mx), jnp.exp(l1 - mx), jnp.exp(l2 - mx)
        num = w0 * os_ref[0, sl, :] + w1 * os_ref[1, sl, :] + w2 * os_ref[2, sl, :]
        o_ref[sl, :] = (num / (w0 + w1 + w2)).astype(o_ref.dtype)
        return carry

    lax.fori_loop(0, seq // rows, combine, 0)


def _dil_bias_table(seq):
    blk = DIL_BLOCK
    i = np.arange(blk)[:, None]
    j = np.arange(2 * blk)[None, :]
    dist = (i - j + blk).astype(np.float64)
    tab = np.zeros((DIL_HEADS // 2, len(DIL_PATTERNS), 2, blk, 2 * blk), np.float32)
    for hp in range(DIL_HEADS // 2):
        for p, (window, dil) in enumerate(DIL_PATTERNS):
            valid = (dist >= 0) & (dist <= window // dil)
            for e in range(2):
                slope = 2.0 ** (-8.0 * (2 * hp + e + 1) / DIL_HEADS)
                tab[hp, p, e] = np.where(valid, -slope * dist * dil, -np.inf)
    return jnp.asarray(tab)


def _dil(z, bias):
    bsz, seq, _ = z.shape
    npair = DIL_HEADS // 2

    def zspec(off):
        return pl.BlockSpec((None, seq, LANES), lambda b, hp: (b, 0, off // LANES + hp))

    return pl.pallas_call(
        _dil_kernel,
        grid=(bsz, npair),
        in_specs=[
            zspec(Z_DQ), zspec(Z_DK), zspec(Z_DV),
            pl.BlockSpec((None,) + bias.shape[1:], lambda b, hp: (hp, 0, 0, 0, 0)),
        ],
        out_specs=pl.BlockSpec((None, seq, LANES), lambda b, hp: (b, 0, hp)),
        out_shape=jax.ShapeDtypeStruct((bsz, seq, DIL_W), BF16),
        scratch_shapes=[pltpu.VMEM((len(DIL_PATTERNS), seq, LANES), F32),
                        pltpu.VMEM((len(DIL_PATTERNS), seq, LANES), F32)],
        compiler_params=_params("parallel", "arbitrary"),
        name="dil_attn",
    )(z, z, z, bias)


def _mlstm_kernel(qk_ref, v_ref, og_ref, sm_ref, cw_ref, cb_ref, bi_ref, bf_ref, gn_ref, o_ref,
                  c_ref, n_ref, m_ref):
    seq = qk_ref.shape[0]
    rows, chunk = MIX_ROWS, ML_CHUNK
    n_chunks = rows // chunk
    c_ref[...] = jnp.zeros(c_ref.shape, F32)
    n_ref[...] = jnp.zeros(n_ref.shape, F32)
    m_ref[...] = jnp.zeros(m_ref.shape, F32)

    same, causal = _chunk_masks(rows, chunk)
    tril = causal.astype(F32)
    ones = same.astype(F32)
    pad = 8

    def tile(i, carry):
        t0 = pl.multiple_of(i * rows, rows)
        sl = pl.ds(t0, rows)
        h0 = pl.multiple_of(jnp.maximum(t0 - pad, 0), pad)
        hist = qk_ref[pl.ds(h0, pad), :] * (t0 > 0).astype(F32)
        xcat = jnp.concatenate([hist, qk_ref[sl, :]], axis=0)
        conv = cb_ref[...]
        for tap in range(ML_CONV):
            off = pad - (ML_CONV - 1) + tap
            conv = conv + xcat[off:off + rows] * cw_ref[tap:tap + 1, :]
        qk = conv * jax.nn.sigmoid(conv)

        sm = sm_ref[sl, :]
        logf = _log_sigmoid(sm + bf_ref[...])
        logi = sm + bi_ref[...]
        f_cum = _dot_exact(tril, logf)
        f_tot = _dot_exact(ones, logf)
        f_cum_t = f_cum.T
        logi_t = logi.T

        for h in range(ML_HEADS):
            hs = slice(h * ML_DH, (h + 1) * ML_DH)
            q = qk[:, hs]
            k = qk[:, ML_W + h * ML_DH:ML_W + (h + 1) * ML_DH] * (ML_DH ** -0.5)
            qb, kb = q.astype(BF16), k.astype(BF16)
            v = v_ref[sl, hs]
            vb = v.astype(BF16)
            fc = f_cum[:, SM_F + h:SM_F + h + 1]
            ft = f_tot[:, SM_F + h:SM_F + h + 1]
            li = logi[:, SM_I + h:SM_I + h + 1]
            fr = f_cum_t[SM_F + h:SM_F + h + 1, :]
            lir = logi_t[SM_I + h:SM_I + h + 1, :]
            dlog = jnp.where(causal, fc - fr + lir, -jnp.inf)
            m_intra = jnp.max(dlog, axis=-1, keepdims=True)
            sqk = _dot_nt(qb, kb) * jnp.exp(dlog - m_intra)
            num_intra = _dot(sqk.astype(BF16), vb)
            den_intra = jnp.sum(sqk, axis=-1, keepdims=True)
            g = ft - fc + li
            outs = []
            for c in range(n_chunks):
                cs = slice(c * chunk, (c + 1) * chunk)
                c_prev, n_prev, m_prev = c_ref[h], n_ref[h], m_ref[h]
                a_t = fc[cs] + m_prev
                m_t = jnp.maximum(a_t, m_intra[cs])
                w_inter = jnp.exp(a_t - m_t)
                w_intra = jnp.exp(m_intra[cs] - m_t)
                num = w_inter * _dot_nt(qb[cs], c_prev.astype(BF16)) + w_intra * num_intra[cs]
                den = (w_inter * jnp.sum(q[cs] * n_prev, axis=-1, keepdims=True)
                       + w_intra * den_intra[cs])
                outs.append(num / jnp.maximum(jnp.abs(den), jnp.exp(-m_t)))
                gc = g[cs]
                m_chunk = jnp.max(gc, axis=0, keepdims=True)
                wgt = jnp.exp(gc - m_chunk)
                kv = _dot_tn((v[cs] * wgt).astype(BF16), kb[cs])
                nk = jnp.sum(k[cs] * wgt, axis=0, keepdims=True)
                f_last = ft[c * chunk:c * chunk + 1]
                m_new = jnp.maximum(f_last + m_prev, m_chunk)
                a = jnp.exp(f_last + m_prev - m_new)
                b = jnp.exp(m_chunk - m_new)
                c_ref[h] = a * c_prev + b * kv
                n_ref[h] = a * n_prev + b * nk
                m_ref[h] = m_new
            hcat = jnp.concatenate(outs, axis=0)
            mu = jnp.mean(hcat, axis=-1, keepdims=True)
            d = hcat - mu
            var = jnp.mean(d * d, axis=-1, keepdims=True)
            hn = d * lax.rsqrt(var + LN_EPS)
            y = hn * gn_ref[:, hs] * jax.nn.sigmoid(og_ref[sl, hs])
            o_ref[sl, hs] = y.astype(o_ref.dtype)
        return carry

    lax.fori_loop(0, seq // rows, tile, 0)


def _mlstm(z, cw, cb, bi, bf, gn, l):
    bsz, seq, _ = z.shape

    def zspec(width, off):
        return pl.BlockSpec((None, seq, width), lambda b: (b, 0, off // width))

    def lspec(r, c):
        return _resident((None, r, c), lambda b: (l, 0, 0))

    return pl.pallas_call(
        _mlstm_kernel,
        grid=(bsz,),
        in_specs=[
            zspec(2 * ML_W, Z_MQK), zspec(ML_W, Z_MV), zspec(ML_W, Z_MO), zspec(LANES, Z_SMALL),
            lspec(ML_CONV, 2 * ML_W), lspec(1, 2 * ML_W), lspec(1, LANES), lspec(1, LANES), lspec(1, ML_W),
        ],
        out_specs=pl.BlockSpec((None, seq, ML_W), lambda b: (b, 0, 0)),
        out_shape=jax.ShapeDtypeStruct((bsz, seq, ML_W), BF16),
        scratch_shapes=[pltpu.VMEM((ML_HEADS, ML_DH, ML_DH), F32),
                        pltpu.VMEM((ML_HEADS, 1, ML_DH), F32),
                        pltpu.VMEM((ML_HEADS, 1, 1), F32)],
        compiler_params=_params("parallel"),
        name="mlstm",
    )(z, z, z, z, cw, cb, bi, bf, gn)


def _merge_ln_kernel(x_ref, yg_ref, yd_ref, ym_ref, wgt_ref, bg_ref, wpg_ref, wpd_ref, wpm_ref,
                     wo_ref, g_ref, b_ref, o_ref):
    x = x_ref[...]
    gates = jax.nn.sigmoid(_dot(x.astype(BF16), wgt_ref[...]) + bg_ref[...])
    d = D_MODEL
    merged = (gates[:, 0:d] * _dot(yg_ref[...], wpg_ref[...])
              + gates[:, d:2 * d] * _dot(yd_ref[...], wpd_ref[...])
              + gates[:, 2 * d:3 * d] * _dot(ym_ref[...], wpm_ref[...]))
    mix = _dot(merged.astype(BF16), wo_ref[...])
    o_ref[...] = _ln(DN_ALPHA * x + mix, g_ref[...], b_ref[...])


def _merge_ln(x2d, yg, yd, ym, wgate, bgate, wpg, wpd, wpm, wo, g, b, l):
    m = x2d.shape[0]

    def row(width):
        return pl.BlockSpec((TM, width), lambda i: (i, 0))

    def lspec(r, c):
        return _resident((None, r, c), lambda i: (l, 0, 0))

    return pl.pallas_call(
        _merge_ln_kernel,
        grid=(m // TM,),
        in_specs=[
            row(D_MODEL), row(GLA_V_W), row(DIL_W), row(ML_W),
            lspec(D_MODEL, 3 * D_MODEL), lspec(1, 3 * D_MODEL),
            lspec(GLA_V_W, D_MODEL), lspec(DIL_W, D_MODEL), lspec(ML_W, D_MODEL),
            lspec(D_MODEL, D_MODEL), lspec(1, D_MODEL), lspec(1, D_MODEL),
        ],
        out_specs=row(D_MODEL),
        out_shape=jax.ShapeDtypeStruct((m, D_MODEL), F32),
        compiler_params=_params("parallel"),
        name="merge_ln",
    )(x2d, yg, yd, ym, wgate, bgate, wpg, wpd, wpm, wo, g, b)


def _stage_w_in(w_in):
    def cols(idx):
        return w_in[:, :, IN_OFFS[idx]:IN_OFFS[idx + 1]]

    depth = w_in.shape[0]
    small = jnp.concatenate(
        [cols(4), cols(11), cols(12),
         jnp.zeros((depth, D_MODEL, LANES - GLA_RANK - 2 * ML_HEADS), w_in.dtype)], axis=-1)
    wz = jnp.concatenate([cols(0), cols(1), cols(2), cols(3), cols(5), cols(6), cols(7),
                          cols(8), cols(9), cols(10), cols(13), small], axis=-1)
    return wz.astype(BF16), cols(14).astype(BF16)


def _lane_vec(vals, lane0):
    depth, n = vals.shape
    out = jnp.zeros((depth, 1, LANES), F32)
    return out.at[:, 0, lane0:lane0 + n].set(vals.astype(F32))


def kernel(x, ffn1_w_gate, ffn1_w_up, ffn1_w_down, ln1_g, ln1_b, w_in, gla_w_a2, gla_b_a, gla_norm_g,
           ml_conv_w, ml_conv_b, ml_b_i, ml_b_f, ml_norm_g, w_proj_gla, w_proj_dil, w_proj_ml, b_gate,
           w_out, ln2_g, ln2_b, ffn2_w_gate, ffn2_w_up, ffn2_w_down, ln3_g, ln3_b):
    bsz, seq, d = x.shape
    depth = w_in.shape[0]
    assert d == D_MODEL and (bsz * seq) % TM == 0 and seq % (16 * DIL_BLOCK) == 0

    bf = lambda w: w.astype(BF16)
    row = lambda v: v.astype(F32)[:, None, :]
    f1g, f1u, f1d = bf(ffn1_w_gate), bf(ffn1_w_up), bf(ffn1_w_down)
    f2g, f2u, f2d = bf(ffn2_w_gate), bf(ffn2_w_up), bf(ffn2_w_down)
    wz, wgate = _stage_w_in(w_in)
    wa = jnp.zeros((depth, LANES, GLA_QK_W), BF16).at[:, SM_A:SM_A + GLA_RANK, :].set(bf(gla_w_a2))
    bi_vec, bf_vec = _lane_vec(ml_b_i, SM_I), _lane_vec(ml_b_f, SM_F)
    wpg, wpd, wpm, wo = bf(w_proj_gla), bf(w_proj_dil), bf(w_proj_ml), bf(w_out)
    bias = _dil_bias_table(seq)

    h = x.reshape(bsz * seq, d)
    for l in range(depth):
        h = _ffn_ln(h, f1g, f1u, f1d, row(ln1_g), row(ln1_b), l)
        z = _in_proj(h, wz, l).reshape(bsz, seq, Z_W)
        y_gla = _gla(z, wa, row(gla_b_a), row(gla_norm_g), l)
        y_dil = _dil(z, bias)
        y_ml = _mlstm(z, ml_conv_w.astype(F32), row(ml_conv_b), bi_vec, bf_vec, row(ml_norm_g), l)
        flat = lambda y: y.reshape(bsz * seq, y.shape[-1])
        h = _merge_ln(h, flat(y_gla), flat(y_dil), flat(y_ml), wgate, row(b_gate), wpg, wpd, wpm, wo,
                      row(ln2_g), row(ln2_b), l)
        h = _ffn_ln(h, f2g, f2u, f2d, row(ln3_g), row(ln3_b), l)
    return h.reshape(bsz, seq, d)
```

```python
import functools
import math

import jax
import jax.numpy as jnp
import numpy as np
from jax import lax
from jax.experimental import pallas as pl
from jax.experimental.pallas import tpu as pltpu

F32 = jnp.float32
BF16 = jnp.bfloat16

D_MODEL = 1024
DEPTH = 4
GLA_HEADS, GLA_DK, GLA_DV, GLA_RANK, GLA_CHUNK = 4, 64, 128, 16, 64
GLA_GATE_TEMP = 16.0
DIL_HEADS, DIL_DH, DIL_BLOCK = 8, 64, 128
DIL_PATTERNS = ((128, 1), (512, 4), (2048, 16))
ML_HEADS, ML_DH, ML_CHUNK, ML_CONV = 4, 128, 64, 4
D_FF = 2816
DN_ALPHA = (2 * DEPTH) ** 0.25
LN_EPS = 1e-5

GLA_QK_W = GLA_HEADS * GLA_DK
GLA_V_W = GLA_HEADS * GLA_DV
DIL_W = DIL_HEADS * DIL_DH
ML_W = ML_HEADS * ML_DH
IN_WIDTHS = (GLA_QK_W, GLA_QK_W, GLA_V_W, GLA_V_W, GLA_RANK, DIL_W, DIL_W, DIL_W,
             ML_W, ML_W, ML_W, ML_HEADS, ML_HEADS, ML_W, 3 * D_MODEL)
IN_OFFS = tuple(int(v) for v in np.cumsum((0,) + IN_WIDTHS))

LANES = 128
VMEM_LIMIT = 56 * 1024 * 1024
TM = 512
DIL_UNROLL = 8
MIX_ROWS = 256

Z_GQ, Z_GK, Z_GV, Z_GR = 0, 256, 512, 1024
Z_DQ, Z_DK, Z_DV = 1536, 2048, 2560
Z_MQK, Z_MV, Z_MO = 3072, 4096, 4608
Z_SMALL = 5120
Z_W = 5248
SM_A, SM_I, SM_F = 0, 16, 20


def _ln(r, g, b):
    mu = jnp.mean(r, axis=-1, keepdims=True)
    d = r - mu
    var = jnp.mean(d * d, axis=-1, keepdims=True)
    return d * lax.rsqrt(var + LN_EPS) * g + b


def _log_sigmoid(x):
    return -(jnp.maximum(-x, 0.0) + jnp.log1p(jnp.exp(-jnp.abs(x))))


def _dot(a, b):
    return jnp.dot(a, b, preferred_element_type=F32)


def _dot_nt(a, b):
    return lax.dot_general(a, b, (((1,), (1,)), ((), ())), preferred_element_type=F32)


def _dot_tn(a, b):
    return lax.dot_general(a, b, (((0,), (0,)), ((), ())), preferred_element_type=F32)


def _chunk_cumsum(tril, x):
    hi = x.astype(BF16)
    r1 = x - hi.astype(F32)
    mid = r1.astype(BF16)
    lo = (r1 - mid.astype(F32)).astype(BF16)
    return _dot(tril, hi) + _dot(tril, mid) + _dot(tril, lo)


def _chunk_last(x, chunk):
    rows, w = x.shape
    return jnp.concatenate(
        [jnp.broadcast_to(x[c * chunk + chunk - 1:(c + 1) * chunk, :], (chunk, w)) for c in range(rows // chunk)],
        axis=0)


def _resident(shape, index_map):
    return pl.BlockSpec(shape, index_map, pipeline_mode=pl.Buffered(1))


def _params(*sem):
    return pltpu.CompilerParams(dimension_semantics=sem, vmem_limit_bytes=VMEM_LIMIT)


def _ffn_ln_kernel(x_ref, wg_ref, wu_ref, wd_ref, g_ref, b_ref, o_ref):
    x = x_ref[...]
    xb = x.astype(BF16)
    g = _dot(xb, wg_ref[...])
    u = _dot(xb, wu_ref[...])
    h = (g * jax.nn.sigmoid(g) * u).astype(BF16)
    y = _dot(h, wd_ref[...])
    o_ref[...] = _ln(DN_ALPHA * x + 0.5 * y, g_ref[...], b_ref[...])


def _ffn_ln(x2d, wg, wu, wd, g, b, l):
    m = x2d.shape[0]
    return pl.pallas_call(
        _ffn_ln_kernel,
        grid=(m // TM,),
        in_specs=[
            pl.BlockSpec((TM, D_MODEL), lambda i: (i, 0)),
            _resident((None, D_MODEL, D_FF), lambda i: (l, 0, 0)),
            _resident((None, D_MODEL, D_FF), lambda i: (l, 0, 0)),
            _resident((None, D_FF, D_MODEL), lambda i: (l, 0, 0)),
            _resident((None, 1, D_MODEL), lambda i: (l, 0, 0)),
            _resident((None, 1, D_MODEL), lambda i: (l, 0, 0)),
        ],
        out_specs=pl.BlockSpec((TM, D_MODEL), lambda i: (i, 0)),
        out_shape=jax.ShapeDtypeStruct((m, D_MODEL), F32),
        compiler_params=_params("parallel"),
        name="ffn_ln",
    )(x2d, wg, wu, wd, g, b)


def _in_proj_kernel(x_ref, w_ref, z_ref):
    z_ref[...] = _dot(x_ref[...].astype(BF16), w_ref[...])


def _in_proj(x2d, wz, l):
    m = x2d.shape[0]
    return pl.pallas_call(
        _in_proj_kernel,
        grid=(m // TM,),
        in_specs=[
            pl.BlockSpec((TM, D_MODEL), lambda i: (i, 0)),
            _resident((None, D_MODEL, Z_W), lambda i: (l, 0, 0)),
        ],
        out_specs=pl.BlockSpec((TM, Z_W), lambda i: (i, 0)),
        out_shape=jax.ShapeDtypeStruct((m, Z_W), F32),
        compiler_params=_params("parallel"),
        name="in_proj",
    )(x2d, wz)


def _chunk_masks(rows, chunk):
    r = lax.broadcasted_iota(jnp.int32, (rows, rows), 0)
    c = lax.broadcasted_iota(jnp.int32, (rows, rows), 1)
    same = (r // chunk) == (c // chunk)
    causal = same & (c <= r)
    return same, causal


def _gla_kernel(q_ref, k_ref, v_ref, r_ref, sm_ref, wa_ref, ba_ref, gn_ref, o_ref, st_ref):
    seq = q_ref.shape[0]
    rows, chunk = MIX_ROWS, GLA_CHUNK
    n_chunks = rows // chunk
    st_ref[...] = jnp.zeros(st_ref.shape, F32)

    _, causal = _chunk_masks(rows, chunk)
    tril = causal.astype(BF16)
    lane = lax.broadcasted_iota(jnp.int32, (1, LANES), 1)
    head_mask = (lane < GLA_DK, lane >= GLA_DK)

    def tile(i, carry):
        t0 = pl.multiple_of(i * rows, rows)
        sl = pl.ds(t0, rows)
        xa = _dot(sm_ref[sl, :].astype(BF16), wa_ref[...]) + ba_ref[...]
        la = _log_sigmoid(xa) * (1.0 / GLA_GATE_TEMP)
        bc = _chunk_cumsum(tril, la)
        bl = _chunk_last(bc, chunk)
        q = q_ref[sl, :] * (GLA_DK ** -0.5)
        k = k_ref[sl, :]
        q_dec = (q * jnp.exp(bc))
        k_inv = (k * jnp.exp(-bc)).astype(BF16)
        k_dec = (k * jnp.exp(bl - bc)).astype(BF16)
        decay = jnp.exp(bl)
        heads, chunks = range(GLA_HEADS), range(n_chunks)
        pr = [slice((h // 2) * LANES, (h // 2 + 1) * LANES) for h in heads]
        hsl = [slice(h * GLA_DV, (h + 1) * GLA_DV) for h in heads]
        csl = [slice(c * chunk, (c + 1) * chunk) for c in chunks]
        qd = [jnp.where(head_mask[h % 2], q_dec[:, pr[h]], 0.0).astype(BF16) for h in heads]
        vh = [v_ref[sl, hsl[h]].astype(BF16) for h in heads]
        att = [jnp.where(causal, _dot_nt(qd[h], k_inv[:, pr[h]]), 0.0).astype(BF16) for h in heads]
        o_intra = [_dot(att[h], vh[h]) for h in heads]
        kv_t = [[_dot_tn(vh[h][cs], k_dec[cs, pr[h]]) for cs in csl] for h in heads]
        s_st = [[st_ref[h]] for h in heads]
        for c in chunks:
            for h in heads:
                s_st[h].append(s_st[h][c] * decay[c * chunk:c * chunk + 1, pr[h]] + kv_t[h][c])
        for h in heads:
            st_ref[h] = s_st[h][n_chunks]
        o_inter = [[_dot_nt(qd[h][cs], s_st[h][c].astype(BF16)) for c, cs in enumerate(csl)] for h in heads]
        for h in heads:
            o = o_intra[h] + jnp.concatenate(o_inter[h], axis=0)
            mu = jnp.mean(o, axis=-1, keepdims=True)
            d = o - mu
            var = jnp.mean(d * d, axis=-1, keepdims=True)
            hn = d * lax.rsqrt(var + LN_EPS)
            r = r_ref[sl, hsl[h]]
            y = hn * gn_ref[:, hsl[h]] * (r * jax.nn.sigmoid(r))
            o_ref[sl, hsl[h]] = y.astype(o_ref.dtype)
        return carry

    lax.fori_loop(0, seq // rows, tile, 0)


def _gla(z, wa, ba, gn, l):
    bsz, seq, _ = z.shape

    def zspec(width, off):
        return pl.BlockSpec((None, seq, width), lambda b: (b, 0, off // width))

    return pl.pallas_call(
        _gla_kernel,
        grid=(bsz,),
        in_specs=[
            zspec(GLA_QK_W, Z_GQ), zspec(GLA_QK_W, Z_GK), zspec(GLA_V_W, Z_GV), zspec(GLA_V_W, Z_GR),
            zspec(LANES, Z_SMALL),
            _resident((None, LANES, GLA_QK_W), lambda b: (l, 0, 0)),
            _resident((None, 1, GLA_QK_W), lambda b: (l, 0, 0)),
            _resident((None, 1, GLA_V_W), lambda b: (l, 0, 0)),
        ],
        out_specs=pl.BlockSpec((None, seq, GLA_V_W), lambda b: (b, 0, 0)),
        out_shape=jax.ShapeDtypeStruct((bsz, seq, GLA_V_W), BF16),
        scratch_shapes=[pltpu.VMEM((GLA_HEADS, GLA_DV, LANES), F32)],
        compiler_params=_params("parallel"),
        name="gla",
    )(z, z, z, z, z, wa, ba, gn)


def _dil_kernel(q_ref, k_ref, v_ref, bias_ref, o_ref, os_ref, ls_ref):
    seq = q_ref.shape[0]
    blk = DIL_BLOCK
    lane = lax.broadcasted_iota(jnp.int32, (1, LANES), 1)
    lo = lane < DIL_DH
    scale = DIL_DH ** -0.5

    def load(dil, start, pstart):
        qb = q_ref[pl.ds(start, blk, stride=dil), :] * scale
        q2 = jnp.concatenate([jnp.where(lo, qb, 0.0), jnp.where(lo, 0.0, qb)], axis=0).astype(BF16)
        kc = k_ref[pl.ds(start, blk, stride=dil), :]
        vc = v_ref[pl.ds(start, blk, stride=dil), :]
        if pstart is None:
            return q2, kc.astype(BF16), vc.astype(BF16)
        kp = k_ref[pl.ds(pstart, blk, stride=dil), :]
        vp = v_ref[pl.ds(pstart, blk, stride=dil), :]
        return (q2, jnp.concatenate([kp, kc], axis=0).astype(BF16),
                jnp.concatenate([vp, vc], axis=0).astype(BF16))

    def attend(p, has_prev, q2, kcat, vcat):
        s = _dot_nt(q2, kcat) + (bias_ref[p] if has_prev else bias_ref[p, :, blk:])
        m = jnp.max(s, axis=-1, keepdims=True)
        pe = jnp.exp(s - m)
        lsum = jnp.sum(pe, axis=-1, keepdims=True)
        o2 = _dot(pe.astype(BF16), vcat) * (1.0 / lsum)
        lse2 = m + jnp.log(lsum)
        return jnp.where(lo, o2[:blk], o2[blk:]), jnp.where(lo, lse2[:blk], lse2[blk:])

    def run(p, dil, total, block_of):
        u = max(d for d in range(1, DIL_UNROLL + 1) if total % d == 0)

        def step(g, carry):
            blocks = [block_of(g * u + j) for j in range(u)]
            operands = [load(dil, start, pstart) for start, pstart in blocks]
            results = [attend(p, pstart is not None, *ops) for (_, pstart), ops in zip(blocks, operands)]
            for (start, _), (o, lse) in zip(blocks, results):
                os_ref[p, pl.ds(start, blk, stride=dil), :] = o
                ls_ref[p, pl.ds(start, blk, stride=dil), :] = lse
            return carry

        if total == u:
            step(0, 0)
        else:
            lax.fori_loop(0, total // u, step, 0)

    for p, (_, dil) in enumerate(DIL_PATTERNS):
        nb = seq // dil // blk
        span = dil * blk
        run(p, dil, dil, lambda r: (r, None))
        if nb > 1:
            def later(idx, nb=nb, span=span):
                start = idx // (nb - 1) + (1 + idx % (nb - 1)) * span
                return start, start - span

            run(p, dil, dil * (nb - 1), later)

    rows = MIX_ROWS

    def combine(i, carry):
        sl = pl.ds(pl.multiple_of(i * rows, rows), rows)
        l0, l1, l2 = ls_ref[0, sl, :], ls_ref[1, sl, :], ls_ref[2, sl, :]
        mx = jnp.maximum(jnp.maximum(l0, l1), l2)
        w0, w1, w2 = jnp.exp(l0 - mx), jnp.exp(l1 - mx), jnp.exp(l2 - mx)
        num = w0 * os_ref[0, sl, :] + w1 * os_ref[1, sl, :] + w2 * os_ref[2, sl, :]
        o_ref[sl, :] = (num / (w0 + w1 + w2)).astype(o_ref.dtype)
        return carry

    lax.fori_loop(0, seq // rows, combine, 0)


def _dil_bias_table(seq):
    blk = DIL_BLOCK
    i = np.arange(blk)[:, None]
    j = np.arange(2 * blk)[None, :]
    dist = (i - j + blk).astype(np.float64)
    tab = np.zeros((DIL_HEADS // 2, len(DIL_PATTERNS), 2 * blk, 2 * blk), np.float32)
    for hp in range(DIL_HEADS // 2):
        for p, (window, dil) in enumerate(DIL_PATTERNS):
            valid = (dist >= 0) & (dist <= window // dil)
            for e in range(2):
                slope = 2.0 ** (-8.0 * (2 * hp + e + 1) / DIL_HEADS)
                tab[hp, p, e * blk:(e + 1) * blk] = np.where(valid, -slope * dist * dil, -np.inf)
    return jnp.asarray(tab)


def _dil(z, bias):
    bsz, seq, _ = z.shape
    npair = DIL_HEADS // 2

    def zspec(off):
        return pl.BlockSpec((None, seq, LANES), lambda b, hp: (b, 0, off // LANES + hp))

    return pl.pallas_call(
        _dil_kernel,
        grid=(bsz, npair),
        in_specs=[
            zspec(Z_DQ), zspec(Z_DK), zspec(Z_DV),
            pl.BlockSpec((None,) + bias.shape[1:], lambda b, hp: (hp, 0, 0, 0)),
        ],
        out_specs=pl.BlockSpec((None, seq, LANES), lambda b, hp: (b, 0, hp)),
        out_shape=jax.ShapeDtypeStruct((bsz, seq, DIL_W), BF16),
        scratch_shapes=[pltpu.VMEM((len(DIL_PATTERNS), seq, LANES), F32),
                        pltpu.VMEM((len(DIL_PATTERNS), seq, LANES), F32)],
        compiler_params=_params("parallel", "arbitrary"),
        name="dil_attn",
    )(z, z, z, bias)


def _mlstm_kernel(qk_ref, v_ref, og_ref, sm_ref, cw_ref, cb_ref, bi_ref, bf_ref, gn_ref, o_ref,
                  c_ref, n_ref, m_ref):
    seq = qk_ref.shape[0]
    rows, chunk = MIX_ROWS, ML_CHUNK
    n_chunks = rows // chunk
    c_ref[...] = jnp.zeros(c_ref.shape, F32)
    n_ref[...] = jnp.zeros(n_ref.shape, F32)
    m_ref[...] = jnp.zeros(m_ref.shape, F32)

    _, causal = _chunk_masks(rows, chunk)
    tril = causal.astype(BF16)
    pad = 8

    def tile(i, carry):
        t0 = pl.multiple_of(i * rows, rows)
        sl = pl.ds(t0, rows)
        h0 = pl.multiple_of(jnp.maximum(t0 - pad, 0), pad)
        hist = qk_ref[pl.ds(h0, pad), :] * (t0 > 0).astype(F32)
        xcat = jnp.concatenate([hist, qk_ref[sl, :]], axis=0)
        conv = cb_ref[...]
        for tap in range(ML_CONV):
            off = pad - (ML_CONV - 1) + tap
            conv = conv + xcat[off:off + rows] * cw_ref[tap:tap + 1, :]
        qk = conv * jax.nn.sigmoid(conv)

        sm = sm_ref[sl, :]
        logf = _log_sigmoid(sm + bf_ref[...])
        logi = sm + bi_ref[...]
        f_cum = _chunk_cumsum(tril, logf)
        f_tot = _chunk_last(f_cum, chunk)
        f_cum_t = f_cum.T
        logi_t = logi.T

        heads, chunks = range(ML_HEADS), range(n_chunks)
        hsl = [slice(h * ML_DH, (h + 1) * ML_DH) for h in heads]
        csl = [slice(c * chunk, (c + 1) * chunk) for c in chunks]
        q = [qk[:, hsl[h]] for h in heads]
        k = [qk[:, ML_W + h * ML_DH:ML_W + (h + 1) * ML_DH] * (ML_DH ** -0.5) for h in heads]
        v = [v_ref[sl, hsl[h]] for h in heads]
        qb = [x.astype(BF16) for x in q]
        kb = [x.astype(BF16) for x in k]
        vb = [x.astype(BF16) for x in v]
        fc = [f_cum[:, SM_F + h:SM_F + h + 1] for h in heads]
        ft = [f_tot[:, SM_F + h:SM_F + h + 1] for h in heads]
        li = [logi[:, SM_I + h:SM_I + h + 1] for h in heads]
        fr = [f_cum_t[SM_F + h:SM_F + h + 1, :] for h in heads]
        lir = [logi_t[SM_I + h:SM_I + h + 1, :] for h in heads]
        dlog = [jnp.where(causal, fc[h] - fr[h] + lir[h], -jnp.inf) for h in heads]
        m_intra = [jnp.max(dlog[h], axis=-1, keepdims=True) for h in heads]
        qkt = [_dot_nt(qb[h], kb[h]) for h in heads]
        sqk = [qkt[h] * jnp.exp(dlog[h] - m_intra[h]) for h in heads]
        num_intra = [_dot(sqk[h].astype(BF16), vb[h]) for h in heads]
        den_intra = [jnp.sum(sqk[h], axis=-1, keepdims=True) for h in heads]
        g = [ft[h] - fc[h] + li[h] for h in heads]
        m_chunk = [[jnp.max(g[h][cs], axis=0, keepdims=True) for cs in csl] for h in heads]
        wgt = [[jnp.exp(g[h][cs] - m_chunk[h][c]) for c, cs in enumerate(csl)] for h in heads]
        kv = [[_dot_tn((v[h][cs] * wgt[h][c]).astype(BF16), kb[h][cs]) for c, cs in enumerate(csl)]
              for h in heads]
        nk = [[jnp.sum(k[h][cs] * wgt[h][c], axis=0, keepdims=True) for c, cs in enumerate(csl)]
              for h in heads]
        f_last = [[ft[h][c * chunk:c * chunk + 1] for c in chunks] for h in heads]
        c_st = [[c_ref[h]] for h in heads]
        n_st = [[n_ref[h]] for h in heads]
        m_st = [[m_ref[h]] for h in heads]
        for c in chunks:
            for h in heads:
                m_prev = m_st[h][c]
                m_new = jnp.maximum(f_last[h][c] + m_prev, m_chunk[h][c])
                a = jnp.exp(f_last[h][c] + m_prev - m_new)
                b = jnp.exp(m_chunk[h][c] - m_new)
                c_st[h].append(a * c_st[h][c] + b * kv[h][c])
                n_st[h].append(a * n_st[h][c] + b * nk[h][c])
                m_st[h].append(m_new)
        for h in heads:
            c_ref[h], n_ref[h], m_ref[h] = c_st[h][n_chunks], n_st[h][n_chunks], m_st[h][n_chunks]
        inter = [[_dot_nt(qb[h][cs], c_st[h][c].astype(BF16)) for c, cs in enumerate(csl)] for h in heads]
        qn = [[jnp.sum(q[h][cs] * n_st[h][c], axis=-1, keepdims=True) for c, cs in enumerate(csl)]
              for h in heads]
        for h in heads:
            outs = []
            for c, cs in enumerate(csl):
                a_t = fc[h][cs] + m_st[h][c]
                m_t = jnp.maximum(a_t, m_intra[h][cs])
                w_inter = jnp.exp(a_t - m_t)
                w_intra = jnp.exp(m_intra[h][cs] - m_t)
                num = w_inter * inter[h][c] + w_intra * num_intra[h][cs]
                den = w_inter * qn[h][c] + w_intra * den_intra[h][cs]
                outs.append(num / jnp.maximum(jnp.abs(den), jnp.exp(-m_t)))
            hcat = jnp.concatenate(outs, axis=0)
            mu = jnp.mean(hcat, axis=-1, keepdims=True)
            d = hcat - mu
            var = jnp.mean(d * d, axis=-1, keepdims=True)
            hn = d * lax.rsqrt(var + LN_EPS)
            y = hn * gn_ref[:, hsl[h]] * jax.nn.sigmoid(og_ref[sl, hsl[h]])
            o_ref[sl, hsl[h]] = y.astype(o_ref.dtype)
        return carry

    lax.fori_loop(0, seq // rows, tile, 0)


def _mlstm(z, cw, cb, bi, bf, gn, l):
    bsz, seq, _ = z.shape

    def zspec(width, off):
        return pl.BlockSpec((None, seq, width), lambda b: (b, 0, off // width))

    def lspec(r, c):
        return _resident((None, r, c), lambda b: (l, 0, 0))

    return pl.pallas_call(
        _mlstm_kernel,
        grid=(bsz,),
        in_specs=[
            zspec(2 * ML_W, Z_MQK), zspec(ML_W, Z_MV), zspec(ML_W, Z_MO), zspec(LANES, Z_SMALL),
            lspec(ML_CONV, 2 * ML_W), lspec(1, 2 * ML_W), lspec(1, LANES), lspec(1, LANES), lspec(1, ML_W),
        ],
        out_specs=pl.BlockSpec((None, seq, ML_W), lambda b: (b, 0, 0)),
        out_shape=jax.ShapeDtypeStruct((bsz, seq, ML_W), BF16),
        scratch_shapes=[pltpu.VMEM((ML_HEADS, ML_DH, ML_DH), F32),
                        pltpu.VMEM((ML_HEADS, 1, ML_DH), F32),
                        pltpu.VMEM((ML_HEADS, 1, 1), F32)],
        compiler_params=_params("parallel"),
        name="mlstm",
    )(z, z, z, z, cw, cb, bi, bf, gn)


def _merge_ln_kernel(x_ref, yg_ref, yd_ref, ym_ref, wgt_ref, bg_ref, wpg_ref, wpd_ref, wpm_ref,
                     wo_ref, g_ref, b_ref, o_ref):
    x = x_ref[...]
    gates = jax.nn.sigmoid(_dot(x.astype(BF16), wgt_ref[...]) + bg_ref[...])
    d = D_MODEL
    merged = (gates[:, 0:d] * _dot(yg_ref[...], wpg_ref[...])
              + gates[:, d:2 * d] * _dot(yd_ref[...], wpd_ref[...])
              + gates[:, 2 * d:3 * d] * _dot(ym_ref[...], wpm_ref[...]))
    mix = _dot(merged.astype(BF16), wo_ref[...])
    o_ref[...] = _ln(DN_ALPHA * x + mix, g_ref[...], b_ref[...])


def _merge_ln(x2d, yg, yd, ym, wgate, bgate, wpg, wpd, wpm, wo, g, b, l):
    m = x2d.shape[0]

    def row(width):
        return pl.BlockSpec((TM, width), lambda i: (i, 0))

    def lspec(r, c):
        return _resident((None, r, c), lambda i: (l, 0, 0))

    return pl.pallas_call(
        _merge_ln_kernel,
        grid=(m // TM,),
        in_specs=[
            row(D_MODEL), row(GLA_V_W), row(DIL_W), row(ML_W),
            lspec(D_MODEL, 3 * D_MODEL), lspec(1, 3 * D_MODEL),
            lspec(GLA_V_W, D_MODEL), lspec(DIL_W, D_MODEL), lspec(ML_W, D_MODEL),
            lspec(D_MODEL, D_MODEL), lspec(1, D_MODEL), lspec(1, D_MODEL),
        ],
        out_specs=row(D_MODEL),
        out_shape=jax.ShapeDtypeStruct((m, D_MODEL), F32),
        compiler_params=_params("parallel"),
        name="merge_ln",
    )(x2d, yg, yd, ym, wgate, bgate, wpg, wpd, wpm, wo, g, b)


def _stage_w_in(w_in):
    def cols(idx):
        return w_in[:, :, IN_OFFS[idx]:IN_OFFS[idx + 1]]

    depth = w_in.shape[0]
    small = jnp.concatenate(
        [cols(4), cols(11), cols(12),
         jnp.zeros((depth, D_MODEL, LANES - GLA_RANK - 2 * ML_HEADS), w_in.dtype)], axis=-1)
    wz = jnp.concatenate([cols(0), cols(1), cols(2), cols(3), cols(5), cols(6), cols(7),
                          cols(8), cols(9), cols(10), cols(13), small], axis=-1)
    return wz.astype(BF16), cols(14).astype(BF16)


def _lane_vec(vals, lane0):
    depth, n = vals.shape
    out = jnp.zeros((depth, 1, LANES), F32)
    return out.at[:, 0, lane0:lane0 + n].set(vals.astype(F32))


def kernel(x, ffn1_w_gate, ffn1_w_up, ffn1_w_down, ln1_g, ln1_b, w_in, gla_w_a2, gla_b_a, gla_norm_g,
           ml_conv_w, ml_conv_b, ml_b_i, ml_b_f, ml_norm_g, w_proj_gla, w_proj_dil, w_proj_ml, b_gate,
           w_out, ln2_g, ln2_b, ffn2_w_gate, ffn2_w_up, ffn2_w_down, ln3_g, ln3_b):
    bsz, seq, d = x.shape
    depth = w_in.shape[0]
    assert d == D_MODEL and (bsz * seq) % TM == 0 and seq % (16 * DIL_BLOCK) == 0

    bf = lambda w: w.astype(BF16)
    row = lambda v: v.astype(F32)[:, None, :]
    f1g, f1u, f1d = bf(ffn1_w_gate), bf(ffn1_w_up), bf(ffn1_w_down)
    f2g, f2u, f2d = bf(ffn2_w_gate), bf(ffn2_w_up), bf(ffn2_w_down)
    wz, wgate = _stage_w_in(w_in)
    wa = jnp.zeros((depth, LANES, GLA_QK_W), BF16).at[:, SM_A:SM_A + GLA_RANK, :].set(bf(gla_w_a2))
    bi_vec, bf_vec = _lane_vec(ml_b_i, SM_I), _lane_vec(ml_b_f, SM_F)
    wpg, wpd, wpm, wo = bf(w_proj_gla), bf(w_proj_dil), bf(w_proj_ml), bf(w_out)
    bias = _dil_bias_table(seq)

    h = x.reshape(bsz * seq, d)
    for l in range(depth):
        h = _ffn_ln(h, f1g, f1u, f1d, row(ln1_g), row(ln1_b), l)
        z = _in_proj(h, wz, l).reshape(bsz, seq, Z_W)
        y_gla = _gla(z, wa, row(gla_b_a), row(gla_norm_g), l)
        y_dil = _dil(z, bias)
        y_ml = _mlstm(z, ml_conv_w.astype(F32), row(ml_conv_b), bi_vec, bf_vec, row(ml_norm_g), l)
        flat = lambda y: y.reshape(bsz * seq, y.shape[-1])
        h = _merge_ln(h, flat(y_gla), flat(y_dil), flat(y_ml), wgate, row(b_gate), wpg, wpd, wpm, wo,
                      row(ln2_g), row(ln2_b), l)
        h = _ffn_ln(h, f2g, f2u, f2d, row(ln3_g), row(ln3_b), l)
    return h.reshape(bsz, seq, d)
```

```python
import jax
import jax.numpy as jnp
import numpy as np
from jax import lax
from jax.experimental import pallas as pl
from jax.experimental.pallas import tpu as pltpu

F32 = jnp.float32
BF16 = jnp.bfloat16

D_MODEL = 1024
DEPTH = 4
GLA_HEADS, GLA_DK, GLA_DV, GLA_RANK, GLA_CHUNK = 4, 64, 128, 16, 64
GLA_GATE_TEMP = 16.0
DIL_HEADS, DIL_DH, DIL_BLOCK = 8, 64, 128
DIL_PATTERNS = ((128, 1), (512, 4), (2048, 16))
ML_HEADS, ML_DH, ML_CHUNK, ML_CONV = 4, 128, 64, 4
D_FF = 2816
DN_ALPHA = (2 * DEPTH) ** 0.25
LN_EPS = 1e-5

GLA_QK_W = GLA_HEADS * GLA_DK
GLA_V_W = GLA_HEADS * GLA_DV
DIL_W = DIL_HEADS * DIL_DH
ML_W = ML_HEADS * ML_DH
IN_WIDTHS = (GLA_QK_W, GLA_QK_W, GLA_V_W, GLA_V_W, GLA_RANK, DIL_W, DIL_W, DIL_W,
             ML_W, ML_W, ML_W, ML_HEADS, ML_HEADS, ML_W, 3 * D_MODEL)
IN_OFFS = tuple(int(v) for v in np.cumsum((0,) + IN_WIDTHS))

LANES = 128
SUBLANES = 8
VMEM_LIMIT = 56 * 1024 * 1024
TM = 512
DIL_UNROLL = 8
MIX_ROWS = 256

G_Q, G_K, G_V, G_R, G_SM, G_W = 0, 256, 512, 1024, 1536, 1664
M_QK, M_V, M_O, M_SM, M_W = 0, 1024, 1536, 2048, 2176
SM_A, SM_I, SM_F = 0, 16, 20


def _ln(r, g, b):
    mu = jnp.mean(r, axis=-1, keepdims=True)
    d = r - mu
    var = jnp.mean(d * d, axis=-1, keepdims=True)
    return d * lax.rsqrt(var + LN_EPS) * g + b


def _head_norm(o):
    mu = jnp.mean(o, axis=-1, keepdims=True)
    d = o - mu
    var = jnp.mean(d * d, axis=-1, keepdims=True)
    return d * lax.rsqrt(var + LN_EPS)


def _log_sigmoid(x):
    return -(jnp.maximum(-x, 0.0) + jnp.log1p(jnp.exp(-jnp.abs(x))))


def _dot(a, b):
    return jnp.dot(a, b, preferred_element_type=F32)


def _dot_nt(a, b):
    return lax.dot_general(a, b, (((1,), (1,)), ((), ())), preferred_element_type=F32)


def _dot_tn(a, b):
    return lax.dot_general(a, b, (((0,), (0,)), ((), ())), preferred_element_type=F32)


def _chunk_cumsum(tril, x):
    hi = x.astype(BF16)
    r1 = x - hi.astype(F32)
    mid = r1.astype(BF16)
    lo = (r1 - mid.astype(F32)).astype(BF16)
    return _dot(tril, hi) + _dot(tril, mid) + _dot(tril, lo)


def _chunk_last(x, chunk):
    rows, w = x.shape
    return jnp.concatenate(
        [jnp.broadcast_to(x[c * chunk + chunk - 1:(c + 1) * chunk, :], (chunk, w)) for c in range(rows // chunk)],
        axis=0)


def _chunk_causal(rows, chunk):
    r = lax.broadcasted_iota(jnp.int32, (rows, rows), 0)
    c = lax.broadcasted_iota(jnp.int32, (rows, rows), 1)
    return ((r // chunk) == (c // chunk)) & (c <= r)


def _resident(shape, index_map):
    return pl.BlockSpec(shape, index_map, pipeline_mode=pl.Buffered(1))


def _layer_spec(l, r, c):
    return _resident((None, r, c), lambda *_: (l, 0, 0))


def _params(*sem):
    return pltpu.CompilerParams(dimension_semantics=sem, vmem_limit_bytes=VMEM_LIMIT)


def _ffn_ln_kernel(x_ref, wg_ref, wu_ref, wd_ref, g_ref, b_ref, o_ref, *ob_ref):
    x = x_ref[...]
    xb = x.astype(BF16)
    g = _dot(xb, wg_ref[...])
    u = _dot(xb, wu_ref[...])
    h = (g * jax.nn.sigmoid(g) * u).astype(BF16)
    y = _dot(h, wd_ref[...])
    out = _ln(DN_ALPHA * x + 0.5 * y, g_ref[...], b_ref[...])
    o_ref[...] = out
    for ref in ob_ref:
        ref[...] = out.astype(BF16)


def _ffn_ln(x2d, wg, wu, wd, g, b, l, with_bf16):
    m = x2d.shape[0]
    row = pl.BlockSpec((TM, D_MODEL), lambda i: (i, 0))
    out_shape = [jax.ShapeDtypeStruct((m, D_MODEL), F32)]
    if with_bf16:
        out_shape.append(jax.ShapeDtypeStruct((m, D_MODEL), BF16))
    return pl.pallas_call(
        _ffn_ln_kernel,
        grid=(m // TM,),
        in_specs=[row, _layer_spec(l, D_MODEL, D_FF), _layer_spec(l, D_MODEL, D_FF),
                  _layer_spec(l, D_FF, D_MODEL), _layer_spec(l, 1, D_MODEL), _layer_spec(l, 1, D_MODEL)],
        out_specs=[row] * len(out_shape),
        out_shape=out_shape,
        compiler_params=_params("parallel"),
        name="ffn_ln",
    )(x2d, wg, wu, wd, g, b)


def _dil_proj_kernel(x_ref, w_ref, z_ref):
    z_ref[...] = _dot(x_ref[...], w_ref[...])


def _dil_proj(xb2d, w, l):
    m = xb2d.shape[0]
    return pl.pallas_call(
        _dil_proj_kernel,
        grid=(m // TM,),
        in_specs=[pl.BlockSpec((TM, D_MODEL), lambda i: (i, 0)), _layer_spec(l, D_MODEL, 3 * DIL_W)],
        out_specs=pl.BlockSpec((TM, 3 * DIL_W), lambda i: (i, 0)),
        out_shape=jax.ShapeDtypeStruct((m, 3 * DIL_W), F32),
        compiler_params=_params("parallel"),
        name="dil_proj",
    )(xb2d, w)


def _gla_kernel(x_ref, win_ref, wa_ref, ba_ref, gn_ref, wg_ref, bg_ref, wp_ref, o_ref, st_ref):
    seq = x_ref.shape[0]
    rows, chunk = MIX_ROWS, GLA_CHUNK
    n_chunks = rows // chunk
    st_ref[...] = jnp.zeros(st_ref.shape, F32)

    causal = _chunk_causal(rows, chunk)
    tril = causal.astype(BF16)
    lane = lax.broadcasted_iota(jnp.int32, (1, LANES), 1)
    head_mask = (lane < GLA_DK, lane >= GLA_DK)

    def tile(i, carry):
        sl = pl.ds(pl.multiple_of(i * rows, rows), rows)
        xt = x_ref[sl, :]
        z = _dot(xt, win_ref[...])
        gate = jax.nn.sigmoid(_dot(xt, wg_ref[...]) + bg_ref[...])
        xa = _dot(z[:, G_SM:G_W].astype(BF16), wa_ref[...]) + ba_ref[...]
        la = _log_sigmoid(xa) * (1.0 / GLA_GATE_TEMP)
        bc = _chunk_cumsum(tril, la)
        bl = _chunk_last(bc, chunk)
        q = z[:, G_Q:G_K] * (GLA_DK ** -0.5)
        k = z[:, G_K:G_V]
        q_dec = (q * jnp.exp(bc))
        k_inv = (k * jnp.exp(-bc)).astype(BF16)
        k_dec = (k * jnp.exp(bl - bc)).astype(BF16)
        decay = jnp.exp(bl)
        heads, chunks = range(GLA_HEADS), range(n_chunks)
        pr = [slice((h // 2) * LANES, (h // 2 + 1) * LANES) for h in heads]
        csl = [slice(c * chunk, (c + 1) * chunk) for c in chunks]
        qd = [jnp.where(head_mask[h % 2], q_dec[:, pr[h]], 0.0).astype(BF16) for h in heads]
        vh = [z[:, G_V + h * GLA_DV:G_V + (h + 1) * GLA_DV].astype(BF16) for h in heads]
        att = [jnp.where(causal, _dot_nt(qd[h], k_inv[:, pr[h]]), 0.0).astype(BF16) for h in heads]
        o_intra = [_dot(att[h], vh[h]) for h in heads]
        kv_t = [[_dot_tn(vh[h][cs], k_dec[cs, pr[h]]) for cs in csl] for h in heads]
        s_st = [[st_ref[h]] for h in heads]
        for c in chunks:
            for h in heads:
                s_st[h].append(s_st[h][c] * decay[c * chunk:c * chunk + 1, pr[h]] + kv_t[h][c])
        for h in heads:
            st_ref[h] = s_st[h][n_chunks]
        o_inter = [[_dot_nt(qd[h][cs], s_st[h][c].astype(BF16)) for c, cs in enumerate(csl)] for h in heads]
        ys = []
        for h in heads:
            hn = _head_norm(o_intra[h] + jnp.concatenate(o_inter[h], axis=0))
            hs = slice(h * GLA_DV, (h + 1) * GLA_DV)
            r = z[:, G_R + h * GLA_DV:G_R + (h + 1) * GLA_DV]
            ys.append((hn * gn_ref[:, hs] * (r * jax.nn.sigmoid(r))).astype(BF16))
        y = jnp.concatenate(ys, axis=1)
        o_ref[sl, :] = (gate * _dot(y, wp_ref[...])).astype(o_ref.dtype)
        return carry

    lax.fori_loop(0, seq // rows, tile, 0)


def _gla(xb, win, wa, ba, gn, wg, bg, wp, l):
    bsz, seq, _ = xb.shape
    xspec = pl.BlockSpec((None, seq, D_MODEL), lambda b: (b, 0, 0))
    return pl.pallas_call(
        _gla_kernel,
        grid=(bsz,),
        in_specs=[xspec, _layer_spec(l, D_MODEL, G_W), _layer_spec(l, LANES, GLA_QK_W),
                  _layer_spec(l, 1, GLA_QK_W), _layer_spec(l, 1, GLA_V_W),
                  _layer_spec(l, D_MODEL, D_MODEL), _layer_spec(l, 1, D_MODEL),
                  _layer_spec(l, GLA_V_W, D_MODEL)],
        out_specs=xspec,
        out_shape=jax.ShapeDtypeStruct((bsz, seq, D_MODEL), BF16),
        scratch_shapes=[pltpu.VMEM((GLA_HEADS, GLA_DV, LANES), F32)],
        compiler_params=_params("parallel"),
        name="gla",
    )(xb, win, wa, ba, gn, wg, bg, wp)


def _dil_kernel(q_ref, k_ref, v_ref, bias_ref, o_ref, os_ref, ls_ref):
    seq = q_ref.shape[0]
    blk = DIL_BLOCK
    lane = lax.broadcasted_iota(jnp.int32, (1, LANES), 1)
    lo = lane < DIL_DH
    scale = DIL_DH ** -0.5

    def load(dil, start, pstart):
        qb = q_ref[pl.ds(start, blk, stride=dil), :] * scale
        q2 = jnp.concatenate([jnp.where(lo, qb, 0.0), jnp.where(lo, 0.0, qb)], axis=0).astype(BF16)
        kc = k_ref[pl.ds(start, blk, stride=dil), :]
        vc = v_ref[pl.ds(start, blk, stride=dil), :]
        if pstart is None:
            return q2, kc.astype(BF16), vc.astype(BF16)
        kp = k_ref[pl.ds(pstart, blk, stride=dil), :]
        vp = v_ref[pl.ds(pstart, blk, stride=dil), :]
        return (q2, jnp.concatenate([kp, kc], axis=0).astype(BF16),
                jnp.concatenate([vp, vc], axis=0).astype(BF16))

    def run(p, dil, total, block_of):
        u = max(d for d in range(1, DIL_UNROLL + 1) if total % d == 0)

        def step(g, carry):
            blocks = [block_of(g * u + j) for j in range(u)]
            ops = [load(dil, start, pstart) for start, pstart in blocks]
            n = range(u)
            s = [_dot_nt(ops[i][0], ops[i][1])
                 + (bias_ref[p] if blocks[i][1] is not None else bias_ref[p, :, blk:]) for i in n]
            m = [jnp.max(s[i], axis=-1, keepdims=True) for i in n]
            pe = [jnp.exp(s[i] - m[i]) for i in n]
            lsum = [jnp.sum(pe[i], axis=-1, keepdims=True) for i in n]
            o2 = [_dot(pe[i].astype(BF16), ops[i][2]) for i in n]
            num = [jnp.where(lo, o2[i][:blk], o2[i][blk:]) for i in n]
            den = [jnp.where(lo, lsum[i][:blk], lsum[i][blk:]) for i in n]
            mm = [jnp.where(lo, m[i][:blk], m[i][blk:]) for i in n]
            for i in n:
                start = blocks[i][0]
                os_ref[p, pl.ds(start, blk, stride=dil), :] = num[i] / den[i]
                ls_ref[p, pl.ds(start, blk, stride=dil), :] = mm[i] + jnp.log(den[i])
            return carry

        if total == u:
            step(0, 0)
        else:
            lax.fori_loop(0, total // u, step, 0)

    for p, (_, dil) in enumerate(DIL_PATTERNS):
        nb = seq // dil // blk
        span = dil * blk
        run(p, dil, dil, lambda r: (r, None))
        if nb > 1:
            def later(idx, nb=nb, span=span):
                start = idx // (nb - 1) + (1 + idx % (nb - 1)) * span
                return start, start - span

            run(p, dil, dil * (nb - 1), later)

    rows = MIX_ROWS

    def combine(i, carry):
        sl = pl.ds(pl.multiple_of(i * rows, rows), rows)
        l0, l1, l2 = ls_ref[0, sl, :], ls_ref[1, sl, :], ls_ref[2, sl, :]
        mx = jnp.maximum(jnp.maximum(l0, l1), l2)
        w0, w1, w2 = jnp.exp(l0 - mx), jnp.exp(l1 - mx), jnp.exp(l2 - mx)
        num = w0 * os_ref[0, sl, :] + w1 * os_ref[1, sl, :] + w2 * os_ref[2, sl, :]
        o_ref[sl, :] = (num / (w0 + w1 + w2)).astype(o_ref.dtype)
        return carry

    lax.fori_loop(0, seq // rows, combine, 0)


def _dil_bias_table():
    blk = DIL_BLOCK
    i = np.arange(blk)[:, None]
    j = np.arange(2 * blk)[None, :]
    dist = (i - j + blk).astype(np.float64)
    tab = np.zeros((DIL_HEADS // 2, len(DIL_PATTERNS), 2 * blk, 2 * blk), np.float32)
    for hp in range(DIL_HEADS // 2):
        for p, (window, dil) in enumerate(DIL_PATTERNS):
            valid = (dist >= 0) & (dist <= window // dil)
            for e in range(2):
                slope = 2.0 ** (-8.0 * (2 * hp + e + 1) / DIL_HEADS)
                tab[hp, p, e * blk:(e + 1) * blk] = np.where(valid, -slope * dist * dil, -np.inf)
    return jnp.asarray(tab)


def _dil(z, bias):
    bsz, seq, _ = z.shape
    npair = DIL_HEADS // 2

    def zspec(off):
        return pl.BlockSpec((None, seq, LANES), lambda b, hp: (b, 0, off // LANES + hp))

    return pl.pallas_call(
        _dil_kernel,
        grid=(bsz, npair),
        in_specs=[
            zspec(0), zspec(DIL_W), zspec(2 * DIL_W),
            pl.BlockSpec((None,) + bias.shape[1:], lambda b, hp: (hp, 0, 0, 0)),
        ],
        out_specs=pl.BlockSpec((None, seq, LANES), lambda b, hp: (b, 0, hp)),
        out_shape=jax.ShapeDtypeStruct((bsz, seq, DIL_W), BF16),
        scratch_shapes=[pltpu.VMEM((len(DIL_PATTERNS), seq, LANES), F32),
                        pltpu.VMEM((len(DIL_PATTERNS), seq, LANES), F32)],
        compiler_params=_params("parallel", "arbitrary"),
        name="dil_attn",
    )(z, z, z, bias)


def _mlstm_kernel(x_ref, win_ref, cw_ref, cb_ref, bi_ref, bf_ref, gn_ref, wg_ref, bg_ref, wp_ref, o_ref,
                  c_ref, n_ref, m_ref, hist_ref):
    seq = x_ref.shape[0]
    rows, chunk = MIX_ROWS, ML_CHUNK
    n_chunks = rows // chunk
    c_ref[...] = jnp.zeros(c_ref.shape, F32)
    n_ref[...] = jnp.zeros(n_ref.shape, F32)
    m_ref[...] = jnp.zeros(m_ref.shape, F32)
    hist_ref[...] = jnp.zeros(hist_ref.shape, F32)

    causal = _chunk_causal(rows, chunk)
    tril = causal.astype(BF16)
    pad = SUBLANES

    def tile(i, carry):
        sl = pl.ds(pl.multiple_of(i * rows, rows), rows)
        xt = x_ref[sl, :]
        z = _dot(xt, win_ref[...])
        gate = jax.nn.sigmoid(_dot(xt, wg_ref[...]) + bg_ref[...])
        zqk = z[:, M_QK:M_V]
        xcat = jnp.concatenate([hist_ref[...], zqk], axis=0)
        hist_ref[...] = zqk[rows - pad:, :]
        conv = cb_ref[...]
        for tap in range(ML_CONV):
            off = pad - (ML_CONV - 1) + tap
            conv = conv + xcat[off:off + rows] * cw_ref[tap:tap + 1, :]
        qk = conv * jax.nn.sigmoid(conv)

        sm = z[:, M_SM:M_W]
        logf = _log_sigmoid(sm + bf_ref[...])
        logi = sm + bi_ref[...]
        f_cum = _chunk_cumsum(tril, logf)
        f_tot = _chunk_last(f_cum, chunk)
        f_cum_t = f_cum.T
        logi_t = logi.T

        heads, chunks = range(ML_HEADS), range(n_chunks)
        hsl = [slice(h * ML_DH, (h + 1) * ML_DH) for h in heads]
        csl = [slice(c * chunk, (c + 1) * chunk) for c in chunks]
        q = [qk[:, hsl[h]] for h in heads]
        k = [qk[:, ML_W + h * ML_DH:ML_W + (h + 1) * ML_DH] * (ML_DH ** -0.5) for h in heads]
        v = [z[:, M_V + h * ML_DH:M_V + (h + 1) * ML_DH] for h in heads]
        qb = [x.astype(BF16) for x in q]
        kb = [x.astype(BF16) for x in k]
        vb = [x.astype(BF16) for x in v]
        fc = [f_cum[:, SM_F + h:SM_F + h + 1] for h in heads]
        ft = [f_tot[:, SM_F + h:SM_F + h + 1] for h in heads]
        li = [logi[:, SM_I + h:SM_I + h + 1] for h in heads]
        fr = [f_cum_t[SM_F + h:SM_F + h + 1, :] for h in heads]
        lir = [logi_t[SM_I + h:SM_I + h + 1, :] for h in heads]
        dlog = [jnp.where(causal, fc[h] - fr[h] + lir[h], -jnp.inf) for h in heads]
        m_intra = [jnp.max(dlog[h], axis=-1, keepdims=True) for h in heads]
        qkt = [_dot_nt(qb[h], kb[h]) for h in heads]
        sqk = [qkt[h] * jnp.exp(dlog[h] - m_intra[h]) for h in heads]
        num_intra = [_dot(sqk[h].astype(BF16), vb[h]) for h in heads]
        den_intra = [jnp.sum(sqk[h], axis=-1, keepdims=True) for h in heads]
        g = [ft[h] - fc[h] + li[h] for h in heads]
        m_chunk = [[jnp.max(g[h][cs], axis=0, keepdims=True) for cs in csl] for h in heads]
        wgt = [[jnp.exp(g[h][cs] - m_chunk[h][c]) for c, cs in enumerate(csl)] for h in heads]
        kv = [[_dot_tn((v[h][cs] * wgt[h][c]).astype(BF16), kb[h][cs]) for c, cs in enumerate(csl)]
              for h in heads]
        nk = [[jnp.sum(k[h][cs] * wgt[h][c], axis=0, keepdims=True) for c, cs in enumerate(csl)]
              for h in heads]
        f_last = [[ft[h][c * chunk:c * chunk + 1] for c in chunks] for h in heads]
        c_st = [[c_ref[h]] for h in heads]
        n_st = [[n_ref[h]] for h in heads]
        m_st = [[m_ref[h]] for h in heads]
        for c in chunks:
            for h in heads:
                m_prev = m_st[h][c]
                m_new = jnp.maximum(f_last[h][c] + m_prev, m_chunk[h][c])
                a = jnp.exp(f_last[h][c] + m_prev - m_new)
                b = jnp.exp(m_chunk[h][c] - m_new)
                c_st[h].append(a * c_st[h][c] + b * kv[h][c])
                n_st[h].append(a * n_st[h][c] + b * nk[h][c])
                m_st[h].append(m_new)
        for h in heads:
            c_ref[h], n_ref[h], m_ref[h] = c_st[h][n_chunks], n_st[h][n_chunks], m_st[h][n_chunks]
        inter = [[_dot_nt(qb[h][cs], c_st[h][c].astype(BF16)) for c, cs in enumerate(csl)] for h in heads]
        qn = [[jnp.sum(q[h][cs] * n_st[h][c], axis=-1, keepdims=True) for c, cs in enumerate(csl)]
              for h in heads]
        ys = []
        for h in heads:
            outs = []
            for c, cs in enumerate(csl):
                a_t = fc[h][cs] + m_st[h][c]
                m_t = jnp.maximum(a_t, m_intra[h][cs])
                w_inter = jnp.exp(a_t - m_t)
                w_intra = jnp.exp(m_intra[h][cs] - m_t)
                num = w_inter * inter[h][c] + w_intra * num_intra[h][cs]
                den = w_inter * qn[h][c] + w_intra * den_intra[h][cs]
                outs.append(num / jnp.maximum(jnp.abs(den), jnp.exp(-m_t)))
            hn = _head_norm(jnp.concatenate(outs, axis=0))
            og = z[:, M_O + h * ML_DH:M_O + (h + 1) * ML_DH]
            ys.append((hn * gn_ref[:, hsl[h]] * jax.nn.sigmoid(og)).astype(BF16))
        y = jnp.concatenate(ys, axis=1)
        o_ref[sl, :] = (gate * _dot(y, wp_ref[...])).astype(o_ref.dtype)
        return carry

    lax.fori_loop(0, seq // rows, tile, 0)


def _mlstm(xb, win, cw, cb, bi, bf, gn, wg, bg, wp, l):
    bsz, seq, _ = xb.shape
    xspec = pl.BlockSpec((None, seq, D_MODEL), lambda b: (b, 0, 0))
    return pl.pallas_call(
        _mlstm_kernel,
        grid=(bsz,),
        in_specs=[xspec, _layer_spec(l, D_MODEL, M_W), _layer_spec(l, ML_CONV, 2 * ML_W),
                  _layer_spec(l, 1, 2 * ML_W), _layer_spec(l, 1, LANES), _layer_spec(l, 1, LANES),
                  _layer_spec(l, 1, ML_W), _layer_spec(l, D_MODEL, D_MODEL), _layer_spec(l, 1, D_MODEL),
                  _layer_spec(l, ML_W, D_MODEL)],
        out_specs=xspec,
        out_shape=jax.ShapeDtypeStruct((bsz, seq, D_MODEL), BF16),
        scratch_shapes=[pltpu.VMEM((ML_HEADS, ML_DH, ML_DH), F32),
                        pltpu.VMEM((ML_HEADS, 1, ML_DH), F32),
                        pltpu.VMEM((ML_HEADS, 1, 1), F32),
                        pltpu.VMEM((SUBLANES, 2 * ML_W), F32)],
        compiler_params=_params("parallel"),
        name="mlstm",
    )(xb, win, cw, cb, bi, bf, gn, wg, bg, wp)


def _merge_ln_kernel(x_ref, xb_ref, gg_ref, gm_ref, yd_ref, wg_ref, bg_ref, wpd_ref, wo_ref, g_ref, b_ref,
                     o_ref):
    gate_d = jax.nn.sigmoid(_dot(xb_ref[...], wg_ref[...]) + bg_ref[...])
    merged = gg_ref[...].astype(F32) + gm_ref[...].astype(F32) + gate_d * _dot(yd_ref[...], wpd_ref[...])
    mix = _dot(merged.astype(BF16), wo_ref[...])
    o_ref[...] = _ln(DN_ALPHA * x_ref[...] + mix, g_ref[...], b_ref[...])


def _merge_ln(x2d, xb2d, gg, gm, yd, wg, bg, wpd, wo, g, b, l):
    m = x2d.shape[0]

    def row(width):
        return pl.BlockSpec((TM, width), lambda i: (i, 0))

    return pl.pallas_call(
        _merge_ln_kernel,
        grid=(m // TM,),
        in_specs=[row(D_MODEL), row(D_MODEL), row(D_MODEL), row(D_MODEL), row(DIL_W),
                  _layer_spec(l, D_MODEL, D_MODEL), _layer_spec(l, 1, D_MODEL),
                  _layer_spec(l, DIL_W, D_MODEL), _layer_spec(l, D_MODEL, D_MODEL),
                  _layer_spec(l, 1, D_MODEL), _layer_spec(l, 1, D_MODEL)],
        out_specs=row(D_MODEL),
        out_shape=jax.ShapeDtypeStruct((m, D_MODEL), F32),
        compiler_params=_params("parallel"),
        name="merge_ln",
    )(x2d, xb2d, gg, gm, yd, wg, bg, wpd, wo, g, b)


def _stage_w_in(w_in):
    def cols(idx):
        return w_in[:, :, IN_OFFS[idx]:IN_OFFS[idx + 1]]

    depth = w_in.shape[0]
    small = jnp.concatenate(
        [cols(4), cols(11), cols(12),
         jnp.zeros((depth, D_MODEL, LANES - GLA_RANK - 2 * ML_HEADS), w_in.dtype)], axis=-1)
    w_gla = jnp.concatenate([cols(0), cols(1), cols(2), cols(3), small], axis=-1)
    w_dil = jnp.concatenate([cols(5), cols(6), cols(7)], axis=-1)
    w_ml = jnp.concatenate([cols(8), cols(9), cols(10), cols(13), small], axis=-1)
    gates = cols(14).astype(BF16)
    d = D_MODEL
    return (w_gla.astype(BF16), w_dil.astype(BF16), w_ml.astype(BF16),
            gates[:, :, 0:d], gates[:, :, d:2 * d], gates[:, :, 2 * d:3 * d])


def _lane_vec(vals, lane0):
    depth, n = vals.shape
    out = jnp.zeros((depth, 1, LANES), F32)
    return out.at[:, 0, lane0:lane0 + n].set(vals.astype(F32))


def kernel(x, ffn1_w_gate, ffn1_w_up, ffn1_w_down, ln1_g, ln1_b, w_in, gla_w_a2, gla_b_a, gla_norm_g,
           ml_conv_w, ml_conv_b, ml_b_i, ml_b_f, ml_norm_g, w_proj_gla, w_proj_dil, w_proj_ml, b_gate,
           w_out, ln2_g, ln2_b, ffn2_w_gate, ffn2_w_up, ffn2_w_down, ln3_g, ln3_b):
    bsz, seq, d = x.shape
    depth = w_in.shape[0]
    assert d == D_MODEL and (bsz * seq) % TM == 0 and seq % (16 * DIL_BLOCK) == 0

    bf = lambda w: w.astype(BF16)
    row = lambda v: v.astype(F32)[:, None, :]
    f1g, f1u, f1d = bf(ffn1_w_gate), bf(ffn1_w_up), bf(ffn1_w_down)
    f2g, f2u, f2d = bf(ffn2_w_gate), bf(ffn2_w_up), bf(ffn2_w_down)
    w_gla, w_dil, w_ml, wg_gla, wg_dil, wg_ml = _stage_w_in(w_in)
    bg = b_gate.astype(F32)
    bg_gla, bg_dil, bg_ml = row(bg[:, 0:d]), row(bg[:, d:2 * d]), row(bg[:, 2 * d:3 * d])
    wa = jnp.zeros((depth, LANES, GLA_QK_W), BF16).at[:, SM_A:SM_A + GLA_RANK, :].set(bf(gla_w_a2))
    bi_vec, bf_vec = _lane_vec(ml_b_i, SM_I), _lane_vec(ml_b_f, SM_F)
    wpg, wpd, wpm, wo = bf(w_proj_gla), bf(w_proj_dil), bf(w_proj_ml), bf(w_out)
    bias = _dil_bias_table()

    h = x.reshape(bsz * seq, d)
    for l in range(depth):
        h, hb = _ffn_ln(h, f1g, f1u, f1d, row(ln1_g), row(ln1_b), l, True)
        hb3 = hb.reshape(bsz, seq, d)
        g_gla = _gla(hb3, w_gla, wa, row(gla_b_a), row(gla_norm_g), wg_gla, bg_gla, wpg, l)
        z_dil = _dil_proj(hb, w_dil, l).reshape(bsz, seq, 3 * DIL_W)
        y_dil = _dil(z_dil, bias)
        g_ml = _mlstm(hb3, w_ml, ml_conv_w.astype(F32), row(ml_conv_b), bi_vec, bf_vec, row(ml_norm_g),
                      wg_ml, bg_ml, wpm, l)
        flat = lambda y: y.reshape(bsz * seq, y.shape[-1])
        h = _merge_ln(h, hb, flat(g_gla), flat(g_ml), flat(y_dil), wg_dil, bg_dil, wpd, wo,
                      row(ln2_g), row(ln2_b), l)
        (h,) = _ffn_ln(h, f2g, f2u, f2d, row(ln3_g), row(ln3_b), l, False)
    return h.reshape(bsz, seq, d)
```

```python
import jax
import jax.numpy as jnp
import numpy as np
from jax import lax
from jax.experimental import pallas as pl
from jax.experimental.pallas import tpu as pltpu

F32 = jnp.float32
BF16 = jnp.bfloat16

D_MODEL = 1024
DEPTH = 4
GLA_HEADS, GLA_DK, GLA_DV, GLA_RANK, GLA_CHUNK = 4, 64, 128, 16, 64
GLA_GATE_TEMP = 16.0
DIL_HEADS, DIL_DH, DIL_BLOCK = 8, 64, 128
DIL_PATTERNS = ((128, 1), (512, 4), (2048, 16))
ML_HEADS, ML_DH, ML_CHUNK, ML_CONV = 4, 128, 64, 4
D_FF = 2816
DN_ALPHA = (2 * DEPTH) ** 0.25
LN_EPS = 1e-5

GLA_QK_W = GLA_HEADS * GLA_DK
GLA_V_W = GLA_HEADS * GLA_DV
DIL_W = DIL_HEADS * DIL_DH
ML_W = ML_HEADS * ML_DH
IN_WIDTHS = (GLA_QK_W, GLA_QK_W, GLA_V_W, GLA_V_W, GLA_RANK, DIL_W, DIL_W, DIL_W,
             ML_W, ML_W, ML_W, ML_HEADS, ML_HEADS, ML_W, 3 * D_MODEL)
IN_OFFS = tuple(int(v) for v in np.cumsum((0,) + IN_WIDTHS))

LANES = 128
SUBLANES = 8
VMEM_LIMIT = 56 * 1024 * 1024
TM = 512
DIL_UNROLL = 8
MIX_ROWS = 256

G_Q, G_K, G_V, G_R, G_SM, G_W = 0, 256, 512, 1024, 1536, 1664
M_QK, M_V, M_O, M_SM, M_W = 0, 1024, 1536, 2048, 2176
SM_A, SM_I, SM_F = 0, 16, 20


def _ln(r, g, b):
    mu = jnp.mean(r, axis=-1, keepdims=True)
    d = r - mu
    var = jnp.mean(d * d, axis=-1, keepdims=True)
    return d * lax.rsqrt(var + LN_EPS) * g + b


def _head_norm(o):
    mu = jnp.mean(o, axis=-1, keepdims=True)
    d = o - mu
    var = jnp.mean(d * d, axis=-1, keepdims=True)
    return d * lax.rsqrt(var + LN_EPS)


def _log_sigmoid(x):
    return -(jnp.maximum(-x, 0.0) + jnp.log1p(jnp.exp(-jnp.abs(x))))


def _dot(a, b):
    return jnp.dot(a, b, preferred_element_type=F32)


def _dot_nt(a, b):
    return lax.dot_general(a, b, (((1,), (1,)), ((), ())), preferred_element_type=F32)


def _dot_tn(a, b):
    return lax.dot_general(a, b, (((0,), (0,)), ((), ())), preferred_element_type=F32)


def _chunk_cumsum(tril, x):
    hi = x.astype(BF16)
    r1 = x - hi.astype(F32)
    mid = r1.astype(BF16)
    lo = (r1 - mid.astype(F32)).astype(BF16)
    return _dot(tril, hi) + _dot(tril, mid) + _dot(tril, lo)


def _chunk_last(x, chunk):
    rows, w = x.shape
    return jnp.concatenate(
        [jnp.broadcast_to(x[c * chunk + chunk - 1:(c + 1) * chunk, :], (chunk, w)) for c in range(rows // chunk)],
        axis=0)


PROJ_COLS = 256


class _SideWork:
    def __init__(self, jobs):
        self._jobs = list(jobs)

    def step(self, n=1):
        for _ in range(n):
            if self._jobs:
                self._jobs.pop(0)()

    def flush(self):
        self.step(len(self._jobs))


def _proj_jobs(x_ref, w_ref, rows, i, z):
    sl = pl.ds(pl.multiple_of(i * rows, rows), rows)
    row0, width = z.shape[0] - rows, z.shape[1]

    def job(c0):
        c1 = min(c0 + PROJ_COLS, width)
        z[row0:, c0:c1] = _dot(x_ref[sl, :], w_ref[:, c0:c1])

    return [lambda c0=c0: job(c0) for c0 in range(0, width, PROJ_COLS)]


def _tile_pairs(n_tiles, x_ref, w_ref, rows, tile, za, zb):
    assert n_tiles % 2 == 0
    _SideWork(_proj_jobs(x_ref, w_ref, rows, 0, za)).flush()

    def pair(j, carry):
        a = 2 * j
        tile(a, za, zb, _SideWork(_proj_jobs(x_ref, w_ref, rows, a + 1, zb)))
        tile(a + 1, zb, za, _SideWork(_proj_jobs(x_ref, w_ref, rows, jnp.minimum(a + 2, n_tiles - 1), za)))
        return carry

    lax.fori_loop(0, n_tiles // 2, pair, 0)


def _chunk_causal(rows, chunk):
    r = lax.broadcasted_iota(jnp.int32, (rows, rows), 0)
    c = lax.broadcasted_iota(jnp.int32, (rows, rows), 1)
    return ((r // chunk) == (c // chunk)) & (c <= r)


def _resident(shape, index_map):
    return pl.BlockSpec(shape, index_map, pipeline_mode=pl.Buffered(1))


def _layer_spec(l, r, c):
    return _resident((None, r, c), lambda *_: (l, 0, 0))


def _params(*sem):
    return pltpu.CompilerParams(dimension_semantics=sem, vmem_limit_bytes=VMEM_LIMIT)


def _ffn_ln_kernel(x_ref, wg_ref, wu_ref, wd_ref, g_ref, b_ref, o_ref, *ob_ref):
    half = x_ref.shape[0] // 2
    sls = [slice(0, half), slice(half, 2 * half)]
    x = [x_ref[s, :] for s in sls]
    xb = [v.astype(BF16) for v in x]
    g = [_dot(v, wg_ref[...]) for v in xb]
    u = [_dot(v, wu_ref[...]) for v in xb]
    h = [(g[i] * jax.nn.sigmoid(g[i]) * u[i]).astype(BF16) for i in range(2)]
    y = [_dot(v, wd_ref[...]) for v in h]
    for i, s in enumerate(sls):
        out = _ln(DN_ALPHA * x[i] + 0.5 * y[i], g_ref[...], b_ref[...])
        o_ref[s, :] = out
        for ref in ob_ref:
            ref[s, :] = out.astype(BF16)


def _ffn_ln(x2d, wg, wu, wd, g, b, l, with_bf16):
    m = x2d.shape[0]
    row = pl.BlockSpec((TM, D_MODEL), lambda i: (i, 0))
    out_shape = [jax.ShapeDtypeStruct((m, D_MODEL), F32)]
    if with_bf16:
        out_shape.append(jax.ShapeDtypeStruct((m, D_MODEL), BF16))
    return pl.pallas_call(
        _ffn_ln_kernel,
        grid=(m // TM,),
        in_specs=[row, _layer_spec(l, D_MODEL, D_FF), _layer_spec(l, D_MODEL, D_FF),
                  _layer_spec(l, D_FF, D_MODEL), _layer_spec(l, 1, D_MODEL), _layer_spec(l, 1, D_MODEL)],
        out_specs=[row] * len(out_shape),
        out_shape=out_shape,
        compiler_params=_params("parallel"),
        name="ffn_ln",
    )(x2d, wg, wu, wd, g, b)


def _dil_proj_kernel(x_ref, w_ref, z_ref):
    z_ref[...] = _dot(x_ref[...], w_ref[...])


def _dil_proj(xb2d, w, l):
    m = xb2d.shape[0]
    return pl.pallas_call(
        _dil_proj_kernel,
        grid=(m // TM,),
        in_specs=[pl.BlockSpec((TM, D_MODEL), lambda i: (i, 0)), _layer_spec(l, D_MODEL, 3 * DIL_W)],
        out_specs=pl.BlockSpec((TM, 3 * DIL_W), lambda i: (i, 0)),
        out_shape=jax.ShapeDtypeStruct((m, 3 * DIL_W), F32),
        compiler_params=_params("parallel"),
        name="dil_proj",
    )(xb2d, w)


def _gla_kernel(x_ref, win_ref, wa_ref, ba_ref, gn_ref, wg_ref, bg_ref, wp_ref, o_ref, st_ref,
                za_ref, zb_ref):
    seq = x_ref.shape[0]
    rows, chunk = MIX_ROWS, GLA_CHUNK
    n_chunks = rows // chunk
    st_ref[...] = jnp.zeros(st_ref.shape, F32)

    causal = _chunk_causal(rows, chunk)
    tril = causal.astype(BF16)
    lane = lax.broadcasted_iota(jnp.int32, (1, LANES), 1)
    head_mask = (lane < GLA_DK, lane >= GLA_DK)

    def tile(i, z, _, side):
        sl = pl.ds(pl.multiple_of(i * rows, rows), rows)
        xa = _dot(z[:, G_SM:G_W].astype(BF16), wa_ref[...]) + ba_ref[...]
        la = _log_sigmoid(xa) * (1.0 / GLA_GATE_TEMP)
        side.step()
        bc = _chunk_cumsum(tril, la)
        bl = _chunk_last(bc, chunk)
        q = z[:, G_Q:G_K] * (GLA_DK ** -0.5)
        k = z[:, G_K:G_V]
        q_dec = (q * jnp.exp(bc))
        side.step()
        k_inv = (k * jnp.exp(-bc)).astype(BF16)
        k_dec = (k * jnp.exp(bl - bc)).astype(BF16)
        decay = jnp.exp(bl)
        side.step()
        heads, chunks = range(GLA_HEADS), range(n_chunks)
        pr = [slice((h // 2) * LANES, (h // 2 + 1) * LANES) for h in heads]
        csl = [slice(c * chunk, (c + 1) * chunk) for c in chunks]
        qd = [jnp.where(head_mask[h % 2], q_dec[:, pr[h]], 0.0).astype(BF16) for h in heads]
        vh = [z[:, G_V + h * GLA_DV:G_V + (h + 1) * GLA_DV].astype(BF16) for h in heads]
        att = [jnp.where(causal, _dot_nt(qd[h], k_inv[:, pr[h]]), 0.0).astype(BF16) for h in heads]
        side.step()
        o_intra = [_dot(att[h], vh[h]) for h in heads]
        kv_t = [[_dot_tn(vh[h][cs], k_dec[cs, pr[h]]) for cs in csl] for h in heads]
        side.step()
        s_st = [[st_ref[h]] for h in heads]
        for c in chunks:
            for h in heads:
                s_st[h].append(s_st[h][c] * decay[c * chunk:c * chunk + 1, pr[h]] + kv_t[h][c])
        for h in heads:
            st_ref[h] = s_st[h][n_chunks]
        o_inter = [[_dot_nt(qd[h][cs], s_st[h][c].astype(BF16)) for c, cs in enumerate(csl)] for h in heads]
        side.step()
        gate = jax.nn.sigmoid(_dot(x_ref[sl, :], wg_ref[...]) + bg_ref[...])
        ys = []
        for h in heads:
            hn = _head_norm(o_intra[h] + jnp.concatenate(o_inter[h], axis=0))
            hs = slice(h * GLA_DV, (h + 1) * GLA_DV)
            r = z[:, G_R + h * GLA_DV:G_R + (h + 1) * GLA_DV]
            ys.append((hn * gn_ref[:, hs] * (r * jax.nn.sigmoid(r))).astype(BF16))
            side.step()
        side.flush()
        y = jnp.concatenate(ys, axis=1)
        o_ref[sl, :] = (gate * _dot(y, wp_ref[...])).astype(o_ref.dtype)

    _tile_pairs(seq // rows, x_ref, win_ref, rows, tile, za_ref, zb_ref)


def _gla(xb, win, wa, ba, gn, wg, bg, wp, l):
    bsz, seq, _ = xb.shape
    xspec = pl.BlockSpec((None, seq, D_MODEL), lambda b: (b, 0, 0))
    return pl.pallas_call(
        _gla_kernel,
        grid=(bsz,),
        in_specs=[xspec, _layer_spec(l, D_MODEL, G_W), _layer_spec(l, LANES, GLA_QK_W),
                  _layer_spec(l, 1, GLA_QK_W), _layer_spec(l, 1, GLA_V_W),
                  _layer_spec(l, D_MODEL, D_MODEL), _layer_spec(l, 1, D_MODEL),
                  _layer_spec(l, GLA_V_W, D_MODEL)],
        out_specs=xspec,
        out_shape=jax.ShapeDtypeStruct((bsz, seq, D_MODEL), BF16),
        scratch_shapes=[pltpu.VMEM((GLA_HEADS, GLA_DV, LANES), F32),
                        pltpu.VMEM((MIX_ROWS, G_W), F32), pltpu.VMEM((MIX_ROWS, G_W), F32)],
        compiler_params=_params("parallel"),
        name="gla",
    )(xb, win, wa, ba, gn, wg, bg, wp)


def _dil_kernel(q_ref, k_ref, v_ref, bias_ref, o_ref, os_ref, ls_ref):
    seq = q_ref.shape[0]
    blk = DIL_BLOCK
    lane = lax.broadcasted_iota(jnp.int32, (1, LANES), 1)
    lo = lane < DIL_DH
    scale = DIL_DH ** -0.5

    def load(dil, start, pstart):
        qb = q_ref[pl.ds(start, blk, stride=dil), :] * scale
        q2 = jnp.concatenate([jnp.where(lo, qb, 0.0), jnp.where(lo, 0.0, qb)], axis=0).astype(BF16)
        kc = k_ref[pl.ds(start, blk, stride=dil), :]
        vc = v_ref[pl.ds(start, blk, stride=dil), :]
        if pstart is None:
            return q2, kc.astype(BF16), vc.astype(BF16)
        kp = k_ref[pl.ds(pstart, blk, stride=dil), :]
        vp = v_ref[pl.ds(pstart, blk, stride=dil), :]
        return (q2, jnp.concatenate([kp, kc], axis=0).astype(BF16),
                jnp.concatenate([vp, vc], axis=0).astype(BF16))

    def run(p, dil, total, block_of):
        u = max(d for d in range(1, DIL_UNROLL + 1) if total % d == 0)

        def step(g, carry):
            blocks = [block_of(g * u + j) for j in range(u)]
            ops = [load(dil, start, pstart) for start, pstart in blocks]
            n = range(u)
            s = [_dot_nt(ops[i][0], ops[i][1])
                 + (bias_ref[p] if blocks[i][1] is not None else bias_ref[p, :, blk:]) for i in n]
            m = [jnp.max(s[i], axis=-1, keepdims=True) for i in n]
            pe = [jnp.exp(s[i] - m[i]) for i in n]
            lsum = [jnp.sum(pe[i], axis=-1, keepdims=True) for i in n]
            o2 = [_dot(pe[i].astype(BF16), ops[i][2]) for i in n]
            num = [jnp.where(lo, o2[i][:blk], o2[i][blk:]) for i in n]
            den = [jnp.where(lo, lsum[i][:blk], lsum[i][blk:]) for i in n]
            mm = [jnp.where(lo, m[i][:blk], m[i][blk:]) for i in n]
            for i in n:
                start = blocks[i][0]
                os_ref[p, pl.ds(start, blk, stride=dil), :] = num[i] / den[i]
                ls_ref[p, pl.ds(start, blk, stride=dil), :] = mm[i] + jnp.log(den[i])
            return carry

        if total == u:
            step(0, 0)
        else:
            lax.fori_loop(0, total // u, step, 0)

    for p, (_, dil) in enumerate(DIL_PATTERNS):
        nb = seq // dil // blk
        span = dil * blk
        run(p, dil, dil, lambda r: (r, None))
        if nb > 1:
            def later(idx, nb=nb, span=span):
                start = idx // (nb - 1) + (1 + idx % (nb - 1)) * span
                return start, start - span

            run(p, dil, dil * (nb - 1), later)

    rows = MIX_ROWS

    def combine(i, carry):
        sl = pl.ds(pl.multiple_of(i * rows, rows), rows)
        l0, l1, l2 = ls_ref[0, sl, :], ls_ref[1, sl, :], ls_ref[2, sl, :]
        mx = jnp.maximum(jnp.maximum(l0, l1), l2)
        w0, w1, w2 = jnp.exp(l0 - mx), jnp.exp(l1 - mx), jnp.exp(l2 - mx)
        num = w0 * os_ref[0, sl, :] + w1 * os_ref[1, sl, :] + w2 * os_ref[2, sl, :]
        o_ref[sl, :] = (num / (w0 + w1 + w2)).astype(o_ref.dtype)
        return carry

    lax.fori_loop(0, seq // rows, combine, 0)


def _dil_bias_table():
    blk = DIL_BLOCK
    i = np.arange(blk)[:, None]
    j = np.arange(2 * blk)[None, :]
    dist = (i - j + blk).astype(np.float64)
    tab = np.zeros((DIL_HEADS // 2, len(DIL_PATTERNS), 2 * blk, 2 * blk), np.float32)
    for hp in range(DIL_HEADS // 2):
        for p, (window, dil) in enumerate(DIL_PATTERNS):
            valid = (dist >= 0) & (dist <= window // dil)
            for e in range(2):
                slope = 2.0 ** (-8.0 * (2 * hp + e + 1) / DIL_HEADS)
                tab[hp, p, e * blk:(e + 1) * blk] = np.where(valid, -slope * dist * dil, -np.inf)
    return jnp.asarray(tab)


def _dil(z, bias):
    bsz, seq, _ = z.shape
    npair = DIL_HEADS // 2

    def zspec(off):
        return pl.BlockSpec((None, seq, LANES), lambda b, hp: (b, 0, off // LANES + hp))

    return pl.pallas_call(
        _dil_kernel,
        grid=(bsz, npair),
        in_specs=[
            zspec(0), zspec(DIL_W), zspec(2 * DIL_W),
            pl.BlockSpec((None,) + bias.shape[1:], lambda b, hp: (hp, 0, 0, 0)),
        ],
        out_specs=pl.BlockSpec((None, seq, LANES), lambda b, hp: (b, 0, hp)),
        out_shape=jax.ShapeDtypeStruct((bsz, seq, DIL_W), BF16),
        scratch_shapes=[pltpu.VMEM((len(DIL_PATTERNS), seq, LANES), F32),
                        pltpu.VMEM((len(DIL_PATTERNS), seq, LANES), F32)],
        compiler_params=_params("parallel", "arbitrary"),
        name="dil_attn",
    )(z, z, z, bias)


def _mlstm_kernel(x_ref, win_ref, cw_ref, cb_ref, bi_ref, bf_ref, gn_ref, wg_ref, bg_ref, wp_ref, o_ref,
                  c_ref, n_ref, m_ref, za_ref, zb_ref):
    seq = x_ref.shape[0]
    rows, chunk = MIX_ROWS, ML_CHUNK
    n_chunks = rows // chunk
    c_ref[...] = jnp.zeros(c_ref.shape, F32)
    n_ref[...] = jnp.zeros(n_ref.shape, F32)
    m_ref[...] = jnp.zeros(m_ref.shape, F32)
    zb_ref[rows:, :] = jnp.zeros((SUBLANES, M_W), F32)

    causal = _chunk_causal(rows, chunk)
    tril = causal.astype(BF16)
    pad = SUBLANES

    def tile(i, z, z_other, side):
        sl = pl.ds(pl.multiple_of(i * rows, rows), rows)
        z[0:pad, M_QK:M_V] = jnp.where(i > 0, z_other[rows:, M_QK:M_V], 0.0)
        conv = cb_ref[...]
        for tap in range(ML_CONV):
            off = pad - (ML_CONV - 1) + tap
            conv = conv + z[off:off + rows, M_QK:M_V] * cw_ref[tap:tap + 1, :]
        qk = conv * jax.nn.sigmoid(conv)
        side.step()

        sm = z[pad:, M_SM:M_W]
        logf = _log_sigmoid(sm + bf_ref[...])
        logi = sm + bi_ref[...]
        f_cum = _chunk_cumsum(tril, logf)
        f_tot = _chunk_last(f_cum, chunk)
        f_cum_t = f_cum.T
        logi_t = logi.T
        side.step()

        heads, chunks = range(ML_HEADS), range(n_chunks)
        hsl = [slice(h * ML_DH, (h + 1) * ML_DH) for h in heads]
        csl = [slice(c * chunk, (c + 1) * chunk) for c in chunks]
        q = [qk[:, hsl[h]] for h in heads]
        k = [qk[:, ML_W + h * ML_DH:ML_W + (h + 1) * ML_DH] * (ML_DH ** -0.5) for h in heads]
        v = [z[pad:, M_V + h * ML_DH:M_V + (h + 1) * ML_DH] for h in heads]
        qb = [x.astype(BF16) for x in q]
        kb = [x.astype(BF16) for x in k]
        vb = [x.astype(BF16) for x in v]
        fc = [f_cum[:, SM_F + h:SM_F + h + 1] for h in heads]
        ft = [f_tot[:, SM_F + h:SM_F + h + 1] for h in heads]
        li = [logi[:, SM_I + h:SM_I + h + 1] for h in heads]
        fr = [f_cum_t[SM_F + h:SM_F + h + 1, :] for h in heads]
        lir = [logi_t[SM_I + h:SM_I + h + 1, :] for h in heads]
        dlog = [jnp.where(causal, fc[h] - fr[h] + lir[h], -jnp.inf) for h in heads]
        m_intra = [jnp.max(dlog[h], axis=-1, keepdims=True) for h in heads]
        side.step()
        qkt = [_dot_nt(qb[h], kb[h]) for h in heads]
        sqk = [qkt[h] * jnp.exp(dlog[h] - m_intra[h]) for h in heads]
        side.step()
        num_intra = [_dot(sqk[h].astype(BF16), vb[h]) for h in heads]
        den_intra = [jnp.sum(sqk[h], axis=-1, keepdims=True) for h in heads]
        side.step()
        g = [ft[h] - fc[h] + li[h] for h in heads]
        m_chunk = [[jnp.max(g[h][cs], axis=0, keepdims=True) for cs in csl] for h in heads]
        wgt = [[jnp.exp(g[h][cs] - m_chunk[h][c]) for c, cs in enumerate(csl)] for h in heads]
        kv = [[_dot_tn((v[h][cs] * wgt[h][c]).astype(BF16), kb[h][cs]) for c, cs in enumerate(csl)]
              for h in heads]
        nk = [[jnp.sum(k[h][cs] * wgt[h][c], axis=0, keepdims=True) for c, cs in enumerate(csl)]
              for h in heads]
        f_last = [[ft[h][c * chunk:c * chunk + 1] for c in chunks] for h in heads]
        side.step()
        c_st = [[c_ref[h]] for h in heads]
        n_st = [[n_ref[h]] for h in heads]
        m_st = [[m_ref[h]] for h in heads]
        for c in chunks:
            for h in heads:
                m_prev = m_st[h][c]
                m_new = jnp.maximum(f_last[h][c] + m_prev, m_chunk[h][c])
                a = jnp.exp(f_last[h][c] + m_prev - m_new)
                b = jnp.exp(m_chunk[h][c] - m_new)
                c_st[h].append(a * c_st[h][c] + b * kv[h][c])
                n_st[h].append(a * n_st[h][c] + b * nk[h][c])
                m_st[h].append(m_new)
        for h in heads:
            c_ref[h], n_ref[h], m_ref[h] = c_st[h][n_chunks], n_st[h][n_chunks], m_st[h][n_chunks]
        side.step()
        inter = [[_dot_nt(qb[h][cs], c_st[h][c].astype(BF16)) for c, cs in enumerate(csl)] for h in heads]
        qn = [[jnp.sum(q[h][cs] * n_st[h][c], axis=-1, keepdims=True) for c, cs in enumerate(csl)]
              for h in heads]
        side.step()
        gate = jax.nn.sigmoid(_dot(x_ref[sl, :], wg_ref[...]) + bg_ref[...])
        ys = []
        for h in heads:
            side.step()
            outs = []
            for c, cs in enumerate(csl):
                a_t = fc[h][cs] + m_st[h][c]
                m_t = jnp.maximum(a_t, m_intra[h][cs])
                w_inter = jnp.exp(a_t - m_t)
                w_intra = jnp.exp(m_intra[h][cs] - m_t)
                num = w_inter * inter[h][c] + w_intra * num_intra[h][cs]
                den = w_inter * qn[h][c] + w_intra * den_intra[h][cs]
                outs.append(num / jnp.maximum(jnp.abs(den), jnp.exp(-m_t)))
            hn = _head_norm(jnp.concatenate(outs, axis=0))
            og = z[pad:, M_O + h * ML_DH:M_O + (h + 1) * ML_DH]
            ys.append((hn * gn_ref[:, hsl[h]] * jax.nn.sigmoid(og)).astype(BF16))
        side.flush()
        y = jnp.concatenate(ys, axis=1)
        o_ref[sl, :] = (gate * _dot(y, wp_ref[...])).astype(o_ref.dtype)

    _tile_pairs(seq // rows, x_ref, win_ref, rows, tile, za_ref, zb_ref)


def _mlstm(xb, win, cw, cb, bi, bf, gn, wg, bg, wp, l):
    bsz, seq, _ = xb.shape
    xspec = pl.BlockSpec((None, seq, D_MODEL), lambda b: (b, 0, 0))
    return pl.pallas_call(
        _mlstm_kernel,
        grid=(bsz,),
        in_specs=[xspec, _layer_spec(l, D_MODEL, M_W), _layer_spec(l, ML_CONV, 2 * ML_W),
                  _layer_spec(l, 1, 2 * ML_W), _layer_spec(l, 1, LANES), _layer_spec(l, 1, LANES),
                  _layer_spec(l, 1, ML_W), _layer_spec(l, D_MODEL, D_MODEL), _layer_spec(l, 1, D_MODEL),
                  _layer_spec(l, ML_W, D_MODEL)],
        out_specs=xspec,
        out_shape=jax.ShapeDtypeStruct((bsz, seq, D_MODEL), BF16),
        scratch_shapes=[pltpu.VMEM((ML_HEADS, ML_DH, ML_DH), F32),
                        pltpu.VMEM((ML_HEADS, 1, ML_DH), F32),
                        pltpu.VMEM((ML_HEADS, 1, 1), F32),
                        pltpu.VMEM((SUBLANES + MIX_ROWS, M_W), F32),
                        pltpu.VMEM((SUBLANES + MIX_ROWS, M_W), F32)],
        compiler_params=_params("parallel"),
        name="mlstm",
    )(xb, win, cw, cb, bi, bf, gn, wg, bg, wp)


def _merge_ln_kernel(x_ref, xb_ref, gg_ref, gm_ref, yd_ref, wg_ref, bg_ref, wpd_ref, wo_ref, g_ref, b_ref,
                     o_ref):
    gate_d = jax.nn.sigmoid(_dot(xb_ref[...], wg_ref[...]) + bg_ref[...])
    merged = gg_ref[...].astype(F32) + gm_ref[...].astype(F32) + gate_d * _dot(yd_ref[...], wpd_ref[...])
    mix = _dot(merged.astype(BF16), wo_ref[...])
    o_ref[...] = _ln(DN_ALPHA * x_ref[...] + mix, g_ref[...], b_ref[...])


def _merge_ln(x2d, xb2d, gg, gm, yd, wg, bg, wpd, wo, g, b, l):
    m = x2d.shape[0]

    def row(width):
        return pl.BlockSpec((TM, width), lambda i: (i, 0))

    return pl.pallas_call(
        _merge_ln_kernel,
        grid=(m // TM,),
        in_specs=[row(D_MODEL), row(D_MODEL), row(D_MODEL), row(D_MODEL), row(DIL_W),
                  _layer_spec(l, D_MODEL, D_MODEL), _layer_spec(l, 1, D_MODEL),
                  _layer_spec(l, DIL_W, D_MODEL), _layer_spec(l, D_MODEL, D_MODEL),
                  _layer_spec(l, 1, D_MODEL), _layer_spec(l, 1, D_MODEL)],
        out_specs=row(D_MODEL),
        out_shape=jax.ShapeDtypeStruct((m, D_MODEL), F32),
        compiler_params=_params("parallel"),
        name="merge_ln",
    )(x2d, xb2d, gg, gm, yd, wg, bg, wpd, wo, g, b)


def _stage_w_in(w_in):
    def cols(idx):
        return w_in[:, :, IN_OFFS[idx]:IN_OFFS[idx + 1]]

    depth = w_in.shape[0]
    small = jnp.concatenate(
        [cols(4), cols(11), cols(12),
         jnp.zeros((depth, D_MODEL, LANES - GLA_RANK - 2 * ML_HEADS), w_in.dtype)], axis=-1)
    w_gla = jnp.concatenate([cols(0), cols(1), cols(2), cols(3), small], axis=-1)
    w_dil = jnp.concatenate([cols(5), cols(6), cols(7)], axis=-1)
    w_ml = jnp.concatenate([cols(8), cols(9), cols(10), cols(13), small], axis=-1)
    gates = cols(14).astype(BF16)
    d = D_MODEL
    return (w_gla.astype(BF16), w_dil.astype(BF16), w_ml.astype(BF16),
            gates[:, :, 0:d], gates[:, :, d:2 * d], gates[:, :, 2 * d:3 * d])


def _lane_vec(vals, lane0):
    depth, n = vals.shape
    out = jnp.zeros((depth, 1, LANES), F32)
    return out.at[:, 0, lane0:lane0 + n].set(vals.astype(F32))


def kernel(x, ffn1_w_gate, ffn1_w_up, ffn1_w_down, ln1_g, ln1_b, w_in, gla_w_a2, gla_b_a, gla_norm_g,
           ml_conv_w, ml_conv_b, ml_b_i, ml_b_f, ml_norm_g, w_proj_gla, w_proj_dil, w_proj_ml, b_gate,
           w_out, ln2_g, ln2_b, ffn2_w_gate, ffn2_w_up, ffn2_w_down, ln3_g, ln3_b):
    bsz, seq, d = x.shape
    depth = w_in.shape[0]
    assert d == D_MODEL and (bsz * seq) % TM == 0 and seq % (16 * DIL_BLOCK) == 0

    bf = lambda w: w.astype(BF16)
    row = lambda v: v.astype(F32)[:, None, :]
    f1g, f1u, f1d = bf(ffn1_w_gate), bf(ffn1_w_up), bf(ffn1_w_down)
    f2g, f2u, f2d = bf(ffn2_w_gate), bf(ffn2_w_up), bf(ffn2_w_down)
    w_gla, w_dil, w_ml, wg_gla, wg_dil, wg_ml = _stage_w_in(w_in)
    bg = b_gate.astype(F32)
    bg_gla, bg_dil, bg_ml = row(bg[:, 0:d]), row(bg[:, d:2 * d]), row(bg[:, 2 * d:3 * d])
    wa = jnp.zeros((depth, LANES, GLA_QK_W), BF16).at[:, SM_A:SM_A + GLA_RANK, :].set(bf(gla_w_a2))
    bi_vec, bf_vec = _lane_vec(ml_b_i, SM_I), _lane_vec(ml_b_f, SM_F)
    wpg, wpd, wpm, wo = bf(w_proj_gla), bf(w_proj_dil), bf(w_proj_ml), bf(w_out)
    bias = _dil_bias_table()

    h = x.reshape(bsz * seq, d)
    for l in range(depth):
        h, hb = _ffn_ln(h, f1g, f1u, f1d, row(ln1_g), row(ln1_b), l, True)
        hb3 = hb.reshape(bsz, seq, d)
        g_gla = _gla(hb3, w_gla, wa, row(gla_b_a), row(gla_norm_g), wg_gla, bg_gla, wpg, l)
        z_dil = _dil_proj(hb, w_dil, l).reshape(bsz, seq, 3 * DIL_W)
        y_dil = _dil(z_dil, bias)
        g_ml = _mlstm(hb3, w_ml, ml_conv_w.astype(F32), row(ml_conv_b), bi_vec, bf_vec, row(ml_norm_g),
                      wg_ml, bg_ml, wpm, l)
        flat = lambda y: y.reshape(bsz * seq, y.shape[-1])
        h = _merge_ln(h, hb, flat(g_gla), flat(g_ml), flat(y_dil), wg_dil, bg_dil, wpd, wo,
                      row(ln2_g), row(ln2_b), l)
        (h,) = _ffn_ln(h, f2g, f2u, f2d, row(ln3_g), row(ln3_b), l, False)
    return h.reshape(bsz, seq, d)
```

```python
import jax
import jax.numpy as jnp
import numpy as np
from jax import lax
from jax.experimental import pallas as pl
from jax.experimental.pallas import tpu as pltpu

F32 = jnp.float32
BF16 = jnp.bfloat16

D_MODEL = 1024
DEPTH = 4
GLA_HEADS, GLA_DK, GLA_DV, GLA_RANK, GLA_CHUNK = 4, 64, 128, 16, 64
GLA_GATE_TEMP = 16.0
DIL_HEADS, DIL_DH, DIL_BLOCK = 8, 64, 128
DIL_PATTERNS = ((128, 1), (512, 4), (2048, 16))
ML_HEADS, ML_DH, ML_CHUNK, ML_CONV = 4, 128, 64, 4
D_FF = 2816
DN_ALPHA = (2 * DEPTH) ** 0.25
LN_EPS = 1e-5

GLA_QK_W = GLA_HEADS * GLA_DK
GLA_V_W = GLA_HEADS * GLA_DV
DIL_W = DIL_HEADS * DIL_DH
ML_W = ML_HEADS * ML_DH
IN_WIDTHS = (GLA_QK_W, GLA_QK_W, GLA_V_W, GLA_V_W, GLA_RANK, DIL_W, DIL_W, DIL_W,
             ML_W, ML_W, ML_W, ML_HEADS, ML_HEADS, ML_W, 3 * D_MODEL)
IN_OFFS = tuple(int(v) for v in np.cumsum((0,) + IN_WIDTHS))

LANES = 128
SUBLANES = 8
VMEM_LIMIT = 56 * 1024 * 1024
TM = 512
DIL_UNROLL = 16
MIX_ROWS = 256

G_Q, G_K, G_V, G_R, G_SM, G_W = 0, 256, 512, 1024, 1536, 1664
M_QK, M_V, M_O, M_SM, M_W = 0, 1024, 1536, 2048, 2176
SM_A, SM_I, SM_F = 0, 16, 20


def _ln(r, g, b):
    mu = jnp.mean(r, axis=-1, keepdims=True)
    d = r - mu
    var = jnp.mean(d * d, axis=-1, keepdims=True)
    return d * lax.rsqrt(var + LN_EPS) * g + b


def _head_norm(o):
    mu = jnp.mean(o, axis=-1, keepdims=True)
    d = o - mu
    var = jnp.mean(d * d, axis=-1, keepdims=True)
    return d * lax.rsqrt(var + LN_EPS)


def _log_sigmoid(x):
    return -(jnp.maximum(-x, 0.0) + jnp.log1p(jnp.exp(-jnp.abs(x))))


def _dot(a, b):
    return jnp.dot(a, b, preferred_element_type=F32)


def _dot_nt(a, b):
    return lax.dot_general(a, b, (((1,), (1,)), ((), ())), preferred_element_type=F32)


def _dot_tn(a, b):
    return lax.dot_general(a, b, (((0,), (0,)), ((), ())), preferred_element_type=F32)


def _chunk_cumsum(tril, x):
    hi = x.astype(BF16)
    r1 = x - hi.astype(F32)
    mid = r1.astype(BF16)
    lo = (r1 - mid.astype(F32)).astype(BF16)
    return _dot(tril, hi) + _dot(tril, mid) + _dot(tril, lo)


def _chunk_last(x, chunk):
    rows, w = x.shape
    return jnp.concatenate(
        [jnp.broadcast_to(x[c * chunk + chunk - 1:(c + 1) * chunk, :], (chunk, w)) for c in range(rows // chunk)],
        axis=0)


PROJ_COLS = 256


class _SideWork:
    def __init__(self, jobs):
        self._jobs = list(jobs)

    def step(self, n=1):
        for _ in range(n):
            if self._jobs:
                self._jobs.pop(0)()

    def flush(self):
        self.step(len(self._jobs))


def _proj_jobs(x_ref, w_ref, rows, i, z):
    sl = pl.ds(pl.multiple_of(i * rows, rows), rows)
    row0, width = z.shape[0] - rows, z.shape[1]

    def job(c0):
        c1 = min(c0 + PROJ_COLS, width)
        z[row0:, c0:c1] = _dot(x_ref[sl, :], w_ref[:, c0:c1])

    return [lambda c0=c0: job(c0) for c0 in range(0, width, PROJ_COLS)]


def _tile_pairs(n_tiles, x_ref, w_ref, rows, tile, za, zb):
    assert n_tiles % 2 == 0
    _SideWork(_proj_jobs(x_ref, w_ref, rows, 0, za)).flush()

    def pair(j, carry):
        a = 2 * j
        tile(a, za, zb, _SideWork(_proj_jobs(x_ref, w_ref, rows, a + 1, zb)))
        tile(a + 1, zb, za, _SideWork(_proj_jobs(x_ref, w_ref, rows, jnp.minimum(a + 2, n_tiles - 1), za)))
        return carry

    lax.fori_loop(0, n_tiles // 2, pair, 0)


def _chunk_causal(rows, chunk):
    r = lax.broadcasted_iota(jnp.int32, (rows, rows), 0)
    c = lax.broadcasted_iota(jnp.int32, (rows, rows), 1)
    return ((r // chunk) == (c // chunk)) & (c <= r)


def _resident(shape, index_map):
    return pl.BlockSpec(shape, index_map, pipeline_mode=pl.Buffered(1))


def _layer_spec(l, r, c):
    return _resident((None, r, c), lambda *_: (l, 0, 0))


def _params(*sem):
    return pltpu.CompilerParams(dimension_semantics=sem, vmem_limit_bytes=VMEM_LIMIT)


def _ffn_ln_kernel(x_ref, wg_ref, wu_ref, wd_ref, g_ref, b_ref, o_ref, *ob_ref):
    half = x_ref.shape[0] // 2
    sls = [slice(0, half), slice(half, 2 * half)]
    x = [x_ref[s, :] for s in sls]
    xb = [v.astype(BF16) for v in x]
    g = [_dot(v, wg_ref[...]) for v in xb]
    u = [_dot(v, wu_ref[...]) for v in xb]
    h = [(g[i] * jax.nn.sigmoid(g[i]) * u[i]).astype(BF16) for i in range(2)]
    y = [_dot(v, wd_ref[...]) for v in h]
    for i, s in enumerate(sls):
        out = _ln(DN_ALPHA * x[i] + 0.5 * y[i], g_ref[...], b_ref[...])
        o_ref[s, :] = out
        for ref in ob_ref:
            ref[s, :] = out.astype(BF16)


def _ffn_ln(x2d, wg, wu, wd, g, b, l, with_bf16):
    m = x2d.shape[0]
    row = pl.BlockSpec((TM, D_MODEL), lambda i: (i, 0))
    out_shape = [jax.ShapeDtypeStruct((m, D_MODEL), F32)]
    if with_bf16:
        out_shape.append(jax.ShapeDtypeStruct((m, D_MODEL), BF16))
    return pl.pallas_call(
        _ffn_ln_kernel,
        grid=(m // TM,),
        in_specs=[row, _layer_spec(l, D_MODEL, D_FF), _layer_spec(l, D_MODEL, D_FF),
                  _layer_spec(l, D_FF, D_MODEL), _layer_spec(l, 1, D_MODEL), _layer_spec(l, 1, D_MODEL)],
        out_specs=[row] * len(out_shape),
        out_shape=out_shape,
        compiler_params=_params("parallel"),
        name="ffn_ln",
    )(x2d, wg, wu, wd, g, b)


def _dil_proj_kernel(x_ref, w_ref, z_ref):
    z_ref[...] = _dot(x_ref[...], w_ref[...])


def _dil_proj(xb2d, w, l):
    m = xb2d.shape[0]
    return pl.pallas_call(
        _dil_proj_kernel,
        grid=(m // TM,),
        in_specs=[pl.BlockSpec((TM, D_MODEL), lambda i: (i, 0)), _layer_spec(l, D_MODEL, 3 * DIL_W)],
        out_specs=pl.BlockSpec((TM, 3 * DIL_W), lambda i: (i, 0)),
        out_shape=jax.ShapeDtypeStruct((m, 3 * DIL_W), F32),
        compiler_params=_params("parallel"),
        name="dil_proj",
    )(xb2d, w)


def _gla_kernel(x_ref, win_ref, wa_ref, ba_ref, gn_ref, wg_ref, bg_ref, wp_ref, o_ref, st_ref,
                za_ref, zb_ref):
    seq = x_ref.shape[0]
    rows, chunk = MIX_ROWS, GLA_CHUNK
    n_chunks = rows // chunk
    st_ref[...] = jnp.zeros(st_ref.shape, F32)

    causal = _chunk_causal(rows, chunk)
    tril = causal.astype(BF16)
    lane = lax.broadcasted_iota(jnp.int32, (1, LANES), 1)
    head_mask = (lane < GLA_DK, lane >= GLA_DK)

    def tile(i, z, _, side):
        sl = pl.ds(pl.multiple_of(i * rows, rows), rows)
        xa = _dot(z[:, G_SM:G_W].astype(BF16), wa_ref[...]) + ba_ref[...]
        la = _log_sigmoid(xa) * (1.0 / GLA_GATE_TEMP)
        side.step()
        bc = _chunk_cumsum(tril, la)
        bl = _chunk_last(bc, chunk)
        q = z[:, G_Q:G_K] * (GLA_DK ** -0.5)
        k = z[:, G_K:G_V]
        q_dec = (q * jnp.exp(bc))
        side.step()
        k_inv = (k * jnp.exp(-bc)).astype(BF16)
        k_dec = (k * jnp.exp(bl - bc)).astype(BF16)
        decay = jnp.exp(bl)
        side.step()
        heads, chunks = range(GLA_HEADS), range(n_chunks)
        pr = [slice((h // 2) * LANES, (h // 2 + 1) * LANES) for h in heads]
        csl = [slice(c * chunk, (c + 1) * chunk) for c in chunks]
        qd = [jnp.where(head_mask[h % 2], q_dec[:, pr[h]], 0.0).astype(BF16) for h in heads]
        vh = [z[:, G_V + h * GLA_DV:G_V + (h + 1) * GLA_DV].astype(BF16) for h in heads]
        att = [jnp.where(causal, _dot_nt(qd[h], k_inv[:, pr[h]]), 0.0).astype(BF16) for h in heads]
        side.step()
        o_intra = [_dot(att[h], vh[h]) for h in heads]
        kv_t = [[_dot_tn(vh[h][cs], k_dec[cs, pr[h]]) for cs in csl] for h in heads]
        side.step()
        s_st = [[st_ref[h]] for h in heads]
        for c in chunks:
            for h in heads:
                s_st[h].append(s_st[h][c] * decay[c * chunk:c * chunk + 1, pr[h]] + kv_t[h][c])
        for h in heads:
            st_ref[h] = s_st[h][n_chunks]
        o_inter = [[_dot_nt(qd[h][cs], s_st[h][c].astype(BF16)) for c, cs in enumerate(csl)] for h in heads]
        side.step()
        gate = jax.nn.sigmoid(_dot(x_ref[sl, :], wg_ref[...]) + bg_ref[...])
        ys = []
        for h in heads:
            hn = _head_norm(o_intra[h] + jnp.concatenate(o_inter[h], axis=0))
            hs = slice(h * GLA_DV, (h + 1) * GLA_DV)
            r = z[:, G_R + h * GLA_DV:G_R + (h + 1) * GLA_DV]
            ys.append((hn * gn_ref[:, hs] * (r * jax.nn.sigmoid(r))).astype(BF16))
            side.step()
        side.flush()
        y = jnp.concatenate(ys, axis=1)
        o_ref[sl, :] = (gate * _dot(y, wp_ref[...])).astype(o_ref.dtype)

    _tile_pairs(seq // rows, x_ref, win_ref, rows, tile, za_ref, zb_ref)


def _gla(xb, win, wa, ba, gn, wg, bg, wp, l):
    bsz, seq, _ = xb.shape
    xspec = pl.BlockSpec((None, seq, D_MODEL), lambda b: (b, 0, 0))
    return pl.pallas_call(
        _gla_kernel,
        grid=(bsz,),
        in_specs=[xspec, _layer_spec(l, D_MODEL, G_W), _layer_spec(l, LANES, GLA_QK_W),
                  _layer_spec(l, 1, GLA_QK_W), _layer_spec(l, 1, GLA_V_W),
                  _layer_spec(l, D_MODEL, D_MODEL), _layer_spec(l, 1, D_MODEL),
                  _layer_spec(l, GLA_V_W, D_MODEL)],
        out_specs=xspec,
        out_shape=jax.ShapeDtypeStruct((bsz, seq, D_MODEL), BF16),
        scratch_shapes=[pltpu.VMEM((GLA_HEADS, GLA_DV, LANES), F32),
                        pltpu.VMEM((MIX_ROWS, G_W), F32), pltpu.VMEM((MIX_ROWS, G_W), F32)],
        compiler_params=_params("parallel"),
        name="gla",
    )(xb, win, wa, ba, gn, wg, bg, wp)


def _dil_kernel(q_ref, k_ref, v_ref, bias_ref, o_ref, os_ref, ls_ref):
    seq = q_ref.shape[0]
    blk = DIL_BLOCK
    lane = lax.broadcasted_iota(jnp.int32, (1, LANES), 1)
    lo = lane < DIL_DH
    scale = DIL_DH ** -0.5

    def load(dil, start, pstart):
        qb = q_ref[pl.ds(start, blk, stride=dil), :] * scale
        q2 = jnp.concatenate([jnp.where(lo, qb, 0.0), jnp.where(lo, 0.0, qb)], axis=0).astype(BF16)
        kc = k_ref[pl.ds(start, blk, stride=dil), :]
        vc = v_ref[pl.ds(start, blk, stride=dil), :]
        if pstart is None:
            return q2, kc.astype(BF16), vc.astype(BF16)
        kp = k_ref[pl.ds(pstart, blk, stride=dil), :]
        vp = v_ref[pl.ds(pstart, blk, stride=dil), :]
        return (q2, jnp.concatenate([kp, kc], axis=0).astype(BF16),
                jnp.concatenate([vp, vc], axis=0).astype(BF16))

    def run(p, dil, total, block_of):
        u = max(d for d in range(1, DIL_UNROLL + 1) if total % d == 0)

        def step(g, carry):
            blocks = [block_of(g * u + j) for j in range(u)]
            ops = [load(dil, start, pstart) for start, pstart in blocks]
            n = range(u)
            s = [_dot_nt(ops[i][0], ops[i][1])
                 + (bias_ref[p] if blocks[i][1] is not None else bias_ref[p, :, blk:]) for i in n]
            m = [jnp.max(s[i], axis=-1, keepdims=True) for i in n]
            pe = [jnp.exp(s[i] - m[i]) for i in n]
            lsum = [jnp.sum(pe[i], axis=-1, keepdims=True) for i in n]
            o2 = [_dot(pe[i].astype(BF16), ops[i][2]) for i in n]
            num = [jnp.where(lo, o2[i][:blk], o2[i][blk:]) for i in n]
            den = [jnp.where(lo, lsum[i][:blk], lsum[i][blk:]) for i in n]
            mm = [jnp.where(lo, m[i][:blk], m[i][blk:]) for i in n]
            for i in n:
                start = blocks[i][0]
                os_ref[p, pl.ds(start, blk, stride=dil), :] = num[i] / den[i]
                ls_ref[p, pl.ds(start, blk, stride=dil), :] = mm[i] + jnp.log(den[i])
            return carry

        if total == u:
            step(0, 0)
        else:
            lax.fori_loop(0, total // u, step, 0)

    for p, (_, dil) in enumerate(DIL_PATTERNS):
        nb = seq // dil // blk
        span = dil * blk
        run(p, dil, dil, lambda r: (r, None))
        if nb > 1:
            def later(idx, nb=nb, span=span):
                start = idx // (nb - 1) + (1 + idx % (nb - 1)) * span
                return start, start - span

            run(p, dil, dil * (nb - 1), later)

    rows = MIX_ROWS

    def combine(i, carry):
        sl = pl.ds(pl.multiple_of(i * rows, rows), rows)
        l0, l1, l2 = ls_ref[0, sl, :], ls_ref[1, sl, :], ls_ref[2, sl, :]
        mx = jnp.maximum(jnp.maximum(l0, l1), l2)
        w0, w1, w2 = jnp.exp(l0 - mx), jnp.exp(l1 - mx), jnp.exp(l2 - mx)
        num = w0 * os_ref[0, sl, :] + w1 * os_ref[1, sl, :] + w2 * os_ref[2, sl, :]
        o_ref[sl, :] = (num / (w0 + w1 + w2)).astype(o_ref.dtype)
        return carry

    lax.fori_loop(0, seq // rows, combine, 0)


def _dil_bias_table():
    blk = DIL_BLOCK
    i = np.arange(blk)[:, None]
    j = np.arange(2 * blk)[None, :]
    dist = (i - j + blk).astype(np.float64)
    tab = np.zeros((DIL_HEADS // 2, len(DIL_PATTERNS), 2 * blk, 2 * blk), np.float32)
    for hp in range(DIL_HEADS // 2):
        for p, (window, dil) in enumerate(DIL_PATTERNS):
            valid = (dist >= 0) & (dist <= window // dil)
            for e in range(2):
                slope = 2.0 ** (-8.0 * (2 * hp + e + 1) / DIL_HEADS)
                tab[hp, p, e * blk:(e + 1) * blk] = np.where(valid, -slope * dist * dil, -np.inf)
    return jnp.asarray(tab)


def _dil(z, bias):
    bsz, seq, _ = z.shape
    npair = DIL_HEADS // 2

    def zspec(off):
        return pl.BlockSpec((None, seq, LANES), lambda b, hp: (b, 0, off // LANES + hp))

    return pl.pallas_call(
        _dil_kernel,
        grid=(bsz, npair),
        in_specs=[
            zspec(0), zspec(DIL_W), zspec(2 * DIL_W),
            pl.BlockSpec((None,) + bias.shape[1:], lambda b, hp: (hp, 0, 0, 0)),
        ],
        out_specs=pl.BlockSpec((None, seq, LANES), lambda b, hp: (b, 0, hp)),
        out_shape=jax.ShapeDtypeStruct((bsz, seq, DIL_W), BF16),
        scratch_shapes=[pltpu.VMEM((len(DIL_PATTERNS), seq, LANES), F32),
                        pltpu.VMEM((len(DIL_PATTERNS), seq, LANES), F32)],
        compiler_params=_params("parallel", "arbitrary"),
        name="dil_attn",
    )(z, z, z, bias)


def _mlstm_kernel(x_ref, win_ref, cw_ref, cb_ref, bi_ref, bf_ref, gn_ref, wg_ref, bg_ref, wp_ref, o_ref,
                  c_ref, n_ref, m_ref, za_ref, zb_ref):
    seq = x_ref.shape[0]
    rows, chunk = MIX_ROWS, ML_CHUNK
    n_chunks = rows // chunk
    c_ref[...] = jnp.zeros(c_ref.shape, F32)
    n_ref[...] = jnp.zeros(n_ref.shape, F32)
    m_ref[...] = jnp.zeros(m_ref.shape, F32)
    zb_ref[rows:, :] = jnp.zeros((SUBLANES, M_W), F32)

    causal = _chunk_causal(rows, chunk)
    tril = causal.astype(BF16)
    pad = SUBLANES

    def tile(i, z, z_other, side):
        sl = pl.ds(pl.multiple_of(i * rows, rows), rows)
        z[0:pad, M_QK:M_V] = jnp.where(i > 0, z_other[rows:, M_QK:M_V], 0.0)
        conv = cb_ref[...]
        for tap in range(ML_CONV):
            off = pad - (ML_CONV - 1) + tap
            conv = conv + z[off:off + rows, M_QK:M_V] * cw_ref[tap:tap + 1, :]
        qk = conv * jax.nn.sigmoid(conv)
        side.step()

        sm = z[pad:, M_SM:M_W]
        logf = _log_sigmoid(sm + bf_ref[...])
        logi = sm + bi_ref[...]
        f_cum = _chunk_cumsum(tril, logf)
        f_tot = _chunk_last(f_cum, chunk)
        f_cum_t = f_cum.T
        logi_t = logi.T
        side.step()

        heads, chunks = range(ML_HEADS), range(n_chunks)
        hsl = [slice(h * ML_DH, (h + 1) * ML_DH) for h in heads]
        csl = [slice(c * chunk, (c + 1) * chunk) for c in chunks]
        q = [qk[:, hsl[h]] for h in heads]
        k = [qk[:, ML_W + h * ML_DH:ML_W + (h + 1) * ML_DH] * (ML_DH ** -0.5) for h in heads]
        v = [z[pad:, M_V + h * ML_DH:M_V + (h + 1) * ML_DH] for h in heads]
        qb = [x.astype(BF16) for x in q]
        kb = [x.astype(BF16) for x in k]
        vb = [x.astype(BF16) for x in v]
        fc = [f_cum[:, SM_F + h:SM_F + h + 1] for h in heads]
        ft = [f_tot[:, SM_F + h:SM_F + h + 1] for h in heads]
        li = [logi[:, SM_I + h:SM_I + h + 1] for h in heads]
        fr = [f_cum_t[SM_F + h:SM_F + h + 1, :] for h in heads]
        lir = [logi_t[SM_I + h:SM_I + h + 1, :] for h in heads]
        dlog = [jnp.where(causal, fc[h] - fr[h] + lir[h], -jnp.inf) for h in heads]
        m_intra = [jnp.max(dlog[h], axis=-1, keepdims=True) for h in heads]
        side.step()
        qkt = [_dot_nt(qb[h], kb[h]) for h in heads]
        sqk = [qkt[h] * jnp.exp(dlog[h] - m_intra[h]) for h in heads]
        side.step()
        num_intra = [_dot(sqk[h].astype(BF16), vb[h]) for h in heads]
        den_intra = [jnp.sum(sqk[h], axis=-1, keepdims=True) for h in heads]
        side.step()
        g = [ft[h] - fc[h] + li[h] for h in heads]
        m_chunk = [[jnp.max(g[h][cs], axis=0, keepdims=True) for cs in csl] for h in heads]
        wgt = [[jnp.exp(g[h][cs] - m_chunk[h][c]) for c, cs in enumerate(csl)] for h in heads]
        kv = [[_dot_tn((v[h][cs] * wgt[h][c]).astype(BF16), kb[h][cs]) for c, cs in enumerate(csl)]
              for h in heads]
        nk = [[jnp.sum(k[h][cs] * wgt[h][c], axis=0, keepdims=True) for c, cs in enumerate(csl)]
              for h in heads]
        f_last = [[ft[h][c * chunk:c * chunk + 1] for c in chunks] for h in heads]
        side.step()
        c_st = [[c_ref[h]] for h in heads]
        n_st = [[n_ref[h]] for h in heads]
        m_st = [[m_ref[h]] for h in heads]
        for c in chunks:
            for h in heads:
                m_prev = m_st[h][c]
                m_new = jnp.maximum(f_last[h][c] + m_prev, m_chunk[h][c])
                a = jnp.exp(f_last[h][c] + m_prev - m_new)
                b = jnp.exp(m_chunk[h][c] - m_new)
                c_st[h].append(a * c_st[h][c] + b * kv[h][c])
                n_st[h].append(a * n_st[h][c] + b * nk[h][c])
                m_st[h].append(m_new)
        for h in heads:
            c_ref[h], n_ref[h], m_ref[h] = c_st[h][n_chunks], n_st[h][n_chunks], m_st[h][n_chunks]
        side.step()
        inter = [[_dot_nt(qb[h][cs], c_st[h][c].astype(BF16)) for c, cs in enumerate(csl)] for h in heads]
        qn = [[jnp.sum(q[h][cs] * n_st[h][c], axis=-1, keepdims=True) for c, cs in enumerate(csl)]
              for h in heads]
        side.step()
        gate = jax.nn.sigmoid(_dot(x_ref[sl, :], wg_ref[...]) + bg_ref[...])
        ys = []
        for h in heads:
            side.step()
            outs = []
            for c, cs in enumerate(csl):
                a_t = fc[h][cs] + m_st[h][c]
                m_t = jnp.maximum(a_t, m_intra[h][cs])
                w_inter = jnp.exp(a_t - m_t)
                w_intra = jnp.exp(m_intra[h][cs] - m_t)
                num = w_inter * inter[h][c] + w_intra * num_intra[h][cs]
                den = w_inter * qn[h][c] + w_intra * den_intra[h][cs]
                outs.append(num / jnp.maximum(jnp.abs(den), jnp.exp(-m_t)))
            hn = _head_norm(jnp.concatenate(outs, axis=0))
            og = z[pad:, M_O + h * ML_DH:M_O + (h + 1) * ML_DH]
            ys.append((hn * gn_ref[:, hsl[h]] * jax.nn.sigmoid(og)).astype(BF16))
        side.flush()
        y = jnp.concatenate(ys, axis=1)
        o_ref[sl, :] = (gate * _dot(y, wp_ref[...])).astype(o_ref.dtype)

    _tile_pairs(seq // rows, x_ref, win_ref, rows, tile, za_ref, zb_ref)


def _mlstm(xb, win, cw, cb, bi, bf, gn, wg, bg, wp, l):
    bsz, seq, _ = xb.shape
    xspec = pl.BlockSpec((None, seq, D_MODEL), lambda b: (b, 0, 0))
    return pl.pallas_call(
        _mlstm_kernel,
        grid=(bsz,),
        in_specs=[xspec, _layer_spec(l, D_MODEL, M_W), _layer_spec(l, ML_CONV, 2 * ML_W),
                  _layer_spec(l, 1, 2 * ML_W), _layer_spec(l, 1, LANES), _layer_spec(l, 1, LANES),
                  _layer_spec(l, 1, ML_W), _layer_spec(l, D_MODEL, D_MODEL), _layer_spec(l, 1, D_MODEL),
                  _layer_spec(l, ML_W, D_MODEL)],
        out_specs=xspec,
        out_shape=jax.ShapeDtypeStruct((bsz, seq, D_MODEL), BF16),
        scratch_shapes=[pltpu.VMEM((ML_HEADS, ML_DH, ML_DH), F32),
                        pltpu.VMEM((ML_HEADS, 1, ML_DH), F32),
                        pltpu.VMEM((ML_HEADS, 1, 1), F32),
                        pltpu.VMEM((SUBLANES + MIX_ROWS, M_W), F32),
                        pltpu.VMEM((SUBLANES + MIX_ROWS, M_W), F32)],
        compiler_params=_params("parallel"),
        name="mlstm",
    )(xb, win, cw, cb, bi, bf, gn, wg, bg, wp)


def _merge_ln_kernel(x_ref, xb_ref, gg_ref, gm_ref, yd_ref, wg_ref, bg_ref, wpd_ref, wo_ref, g_ref, b_ref,
                     o_ref):
    gate_d = jax.nn.sigmoid(_dot(xb_ref[...], wg_ref[...]) + bg_ref[...])
    merged = gg_ref[...].astype(F32) + gm_ref[...].astype(F32) + gate_d * _dot(yd_ref[...], wpd_ref[...])
    mix = _dot(merged.astype(BF16), wo_ref[...])
    o_ref[...] = _ln(DN_ALPHA * x_ref[...] + mix, g_ref[...], b_ref[...])


def _merge_ln(x2d, xb2d, gg, gm, yd, wg, bg, wpd, wo, g, b, l):
    m = x2d.shape[0]

    def row(width):
        return pl.BlockSpec((TM, width), lambda i: (i, 0))

    return pl.pallas_call(
        _merge_ln_kernel,
        grid=(m // TM,),
        in_specs=[row(D_MODEL), row(D_MODEL), row(D_MODEL), row(D_MODEL), row(DIL_W),
                  _layer_spec(l, D_MODEL, D_MODEL), _layer_spec(l, 1, D_MODEL),
                  _layer_spec(l, DIL_W, D_MODEL), _layer_spec(l, D_MODEL, D_MODEL),
                  _layer_spec(l, 1, D_MODEL), _layer_spec(l, 1, D_MODEL)],
        out_specs=row(D_MODEL),
        out_shape=jax.ShapeDtypeStruct((m, D_MODEL), F32),
        compiler_params=_params("parallel"),
        name="merge_ln",
    )(x2d, xb2d, gg, gm, yd, wg, bg, wpd, wo, g, b)


def _stage_w_in(w_in):
    def cols(idx):
        return w_in[:, :, IN_OFFS[idx]:IN_OFFS[idx + 1]]

    depth = w_in.shape[0]
    small = jnp.concatenate(
        [cols(4), cols(11), cols(12),
         jnp.zeros((depth, D_MODEL, LANES - GLA_RANK - 2 * ML_HEADS), w_in.dtype)], axis=-1)
    w_gla = jnp.concatenate([cols(0), cols(1), cols(2), cols(3), small], axis=-1)
    w_dil = jnp.concatenate([cols(5), cols(6), cols(7)], axis=-1)
    w_ml = jnp.concatenate([cols(8), cols(9), cols(10), cols(13), small], axis=-1)
    gates = cols(14).astype(BF16)
    d = D_MODEL
    return (w_gla.astype(BF16), w_dil.astype(BF16), w_ml.astype(BF16),
            gates[:, :, 0:d], gates[:, :, d:2 * d], gates[:, :, 2 * d:3 * d])


def _lane_vec(vals, lane0):
    depth, n = vals.shape
    out = jnp.zeros((depth, 1, LANES), F32)
    return out.at[:, 0, lane0:lane0 + n].set(vals.astype(F32))


def kernel(x, ffn1_w_gate, ffn1_w_up, ffn1_w_down, ln1_g, ln1_b, w_in, gla_w_a2, gla_b_a, gla_norm_g,
           ml_conv_w, ml_conv_b, ml_b_i, ml_b_f, ml_norm_g, w_proj_gla, w_proj_dil, w_proj_ml, b_gate,
           w_out, ln2_g, ln2_b, ffn2_w_gate, ffn2_w_up, ffn2_w_down, ln3_g, ln3_b):
    bsz, seq, d = x.shape
    depth = w_in.shape[0]
    assert d == D_MODEL and (bsz * seq) % TM == 0 and seq % (16 * DIL_BLOCK) == 0

    bf = lambda w: w.astype(BF16)
    row = lambda v: v.astype(F32)[:, None, :]
    f1g, f1u, f1d = bf(ffn1_w_gate), bf(ffn1_w_up), bf(ffn1_w_down)
    f2g, f2u, f2d = bf(ffn2_w_gate), bf(ffn2_w_up), bf(ffn2_w_down)
    w_gla, w_dil, w_ml, wg_gla, wg_dil, wg_ml = _stage_w_in(w_in)
    bg = b_gate.astype(F32)
    bg_gla, bg_dil, bg_ml = row(bg[:, 0:d]), row(bg[:, d:2 * d]), row(bg[:, 2 * d:3 * d])
    wa = jnp.zeros((depth, LANES, GLA_QK_W), BF16).at[:, SM_A:SM_A + GLA_RANK, :].set(bf(gla_w_a2))
    bi_vec, bf_vec = _lane_vec(ml_b_i, SM_I), _lane_vec(ml_b_f, SM_F)
    wpg, wpd, wpm, wo = bf(w_proj_gla), bf(w_proj_dil), bf(w_proj_ml), bf(w_out)
    bias = _dil_bias_table()

    h = x.reshape(bsz * seq, d)
    for l in range(depth):
        h, hb = _ffn_ln(h, f1g, f1u, f1d, row(ln1_g), row(ln1_b), l, True)
        hb3 = hb.reshape(bsz, seq, d)
        g_gla = _gla(hb3, w_gla, wa, row(gla_b_a), row(gla_norm_g), wg_gla, bg_gla, wpg, l)
        z_dil = _dil_proj(hb, w_dil, l).reshape(bsz, seq, 3 * DIL_W)
        y_dil = _dil(z_dil, bias)
        g_ml = _mlstm(hb3, w_ml, ml_conv_w.astype(F32), row(ml_conv_b), bi_vec, bf_vec, row(ml_norm_g),
                      wg_ml, bg_ml, wpm, l)
        flat = lambda y: y.reshape(bsz * seq, y.shape[-1])
        h = _merge_ln(h, hb, flat(g_gla), flat(g_ml), flat(y_dil), wg_dil, bg_dil, wpd, wo,
                      row(ln2_g), row(ln2_b), l)
        (h,) = _ffn_ln(h, f2g, f2u, f2d, row(ln3_g), row(ln3_b), l, False)
    return h.reshape(bsz, seq, d)
```

```python
import jax
import jax.numpy as jnp
import numpy as np
from jax import lax
from jax.experimental import pallas as pl
from jax.experimental.pallas import tpu as pltpu

F32 = jnp.float32
BF16 = jnp.bfloat16

D_MODEL = 1024
DEPTH = 4
GLA_HEADS, GLA_DK, GLA_DV, GLA_RANK, GLA_CHUNK = 4, 64, 128, 16, 64
GLA_GATE_TEMP = 16.0
DIL_HEADS, DIL_DH, DIL_BLOCK = 8, 64, 128
DIL_PATTERNS = ((128, 1), (512, 4), (2048, 16))
ML_HEADS, ML_DH, ML_CHUNK, ML_CONV = 4, 128, 64, 4
D_FF = 2816
DN_ALPHA = (2 * DEPTH) ** 0.25
LN_EPS = 1e-5

GLA_QK_W = GLA_HEADS * GLA_DK
GLA_V_W = GLA_HEADS * GLA_DV
DIL_W = DIL_HEADS * DIL_DH
ML_W = ML_HEADS * ML_DH
IN_WIDTHS = (GLA_QK_W, GLA_QK_W, GLA_V_W, GLA_V_W, GLA_RANK, DIL_W, DIL_W, DIL_W,
             ML_W, ML_W, ML_W, ML_HEADS, ML_HEADS, ML_W, 3 * D_MODEL)
IN_OFFS = tuple(int(v) for v in np.cumsum((0,) + IN_WIDTHS))

LANES = 128
SUBLANES = 8
VMEM_LIMIT = 56 * 1024 * 1024
TM = 512
DIL_UNROLL = 16
MIX_ROWS = 256

G_Q, G_K, G_V, G_R, G_SM, G_W = 0, 256, 512, 1024, 1536, 1664
M_QK, M_V, M_O, M_SM, M_W = 0, 1024, 1536, 2048, 2176
SM_A, SM_I, SM_F = 0, 16, 20


def _ln(r, g, b):
    mu = jnp.mean(r, axis=-1, keepdims=True)
    d = r - mu
    var = jnp.mean(d * d, axis=-1, keepdims=True)
    return d * lax.rsqrt(var + LN_EPS) * g + b


def _head_norm(o):
    mu = jnp.mean(o, axis=-1, keepdims=True)
    d = o - mu
    var = jnp.mean(d * d, axis=-1, keepdims=True)
    return d * lax.rsqrt(var + LN_EPS)


def _log_sigmoid(x):
    return -(jnp.maximum(-x, 0.0) + jnp.log1p(jnp.exp(-jnp.abs(x))))


def _dot(a, b):
    return jnp.dot(a, b, preferred_element_type=F32)


def _dot_nt(a, b):
    return lax.dot_general(a, b, (((1,), (1,)), ((), ())), preferred_element_type=F32)


def _dot_tn(a, b):
    return lax.dot_general(a, b, (((0,), (0,)), ((), ())), preferred_element_type=F32)


def _chunk_cumsum(tril, x):
    hi = x.astype(BF16)
    r1 = x - hi.astype(F32)
    mid = r1.astype(BF16)
    lo = (r1 - mid.astype(F32)).astype(BF16)
    return _dot(tril, hi) + _dot(tril, mid) + _dot(tril, lo)


def _chunk_last(x, chunk):
    rows, w = x.shape
    return jnp.concatenate(
        [jnp.broadcast_to(x[c * chunk + chunk - 1:(c + 1) * chunk, :], (chunk, w)) for c in range(rows // chunk)],
        axis=0)


PROJ_COLS = 256


class _SideWork:
    def __init__(self, jobs):
        self._jobs = list(jobs)

    def step(self, n=1):
        for _ in range(n):
            if self._jobs:
                self._jobs.pop(0)()

    def flush(self):
        self.step(len(self._jobs))


def _tile_rows(i, rows):
    return pl.ds(i * rows if isinstance(i, int) else pl.multiple_of(i * rows, rows), rows)


def _proj_jobs(x_ref, w_ref, rows, i, z):
    sl = _tile_rows(i, rows)
    row0, width = z.shape[0] - rows, z.shape[1]

    def job(c0):
        c1 = min(c0 + PROJ_COLS, width)
        z[row0:, c0:c1] = _dot(x_ref[sl, :], w_ref[:, c0:c1])

    return [lambda c0=c0: job(c0) for c0 in range(0, width, PROJ_COLS)]


def _tile_pairs(n_tiles, x_ref, xn_ref, w_ref, rows, tile, za, zb):
    assert n_tiles % 2 == 0

    @pl.when(pl.program_id(0) == 0)
    def _():
        _SideWork(_proj_jobs(x_ref, w_ref, rows, 0, za)).flush()

    def pair(a, next_jobs):
        tile(a, za, zb, _SideWork(_proj_jobs(x_ref, w_ref, rows, a + 1, zb)))
        tile(a + 1, zb, za, _SideWork(next_jobs))

    def body(j, carry):
        pair(2 * j, _proj_jobs(x_ref, w_ref, rows, 2 * j + 2, za))
        return carry

    lax.fori_loop(0, n_tiles // 2 - 1, body, 0)
    pair(n_tiles - 2, _proj_jobs(xn_ref, w_ref, rows, 0, za))


def _chunk_causal(rows, chunk):
    r = lax.broadcasted_iota(jnp.int32, (rows, rows), 0)
    c = lax.broadcasted_iota(jnp.int32, (rows, rows), 1)
    return ((r // chunk) == (c // chunk)) & (c <= r)


def _resident(shape, index_map):
    return pl.BlockSpec(shape, index_map, pipeline_mode=pl.Buffered(1))


def _layer_spec(l, r, c):
    return _resident((None, r, c), lambda *_: (l, 0, 0))


def _next_tile_spec(bsz):
    return pl.BlockSpec((None, MIX_ROWS, D_MODEL), lambda b: (jnp.minimum(b + 1, bsz - 1), 0, 0))


def _params(*sem):
    return pltpu.CompilerParams(dimension_semantics=sem, vmem_limit_bytes=VMEM_LIMIT)


def _ffn_ln_kernel(x_ref, wg_ref, wu_ref, wd_ref, g_ref, b_ref, o_ref, *ob_ref):
    half = x_ref.shape[0] // 2
    sls = [slice(0, half), slice(half, 2 * half)]
    x = [x_ref[s, :] for s in sls]
    xb = [v.astype(BF16) for v in x]
    g = [_dot(v, wg_ref[...]) for v in xb]
    u = [_dot(v, wu_ref[...]) for v in xb]
    h = [(g[i] * jax.nn.sigmoid(g[i]) * u[i]).astype(BF16) for i in range(2)]
    y = [_dot(v, wd_ref[...]) for v in h]
    for i, s in enumerate(sls):
        out = _ln(DN_ALPHA * x[i] + 0.5 * y[i], g_ref[...], b_ref[...])
        o_ref[s, :] = out
        for ref in ob_ref:
            ref[s, :] = out.astype(BF16)


def _ffn_ln(x2d, wg, wu, wd, g, b, l, with_bf16):
    m = x2d.shape[0]
    row = pl.BlockSpec((TM, D_MODEL), lambda i: (i, 0))
    out_shape = [jax.ShapeDtypeStruct((m, D_MODEL), F32)]
    if with_bf16:
        out_shape.append(jax.ShapeDtypeStruct((m, D_MODEL), BF16))
    return pl.pallas_call(
        _ffn_ln_kernel,
        grid=(m // TM,),
        in_specs=[row, _layer_spec(l, D_MODEL, D_FF), _layer_spec(l, D_MODEL, D_FF),
                  _layer_spec(l, D_FF, D_MODEL), _layer_spec(l, 1, D_MODEL), _layer_spec(l, 1, D_MODEL)],
        out_specs=[row] * len(out_shape),
        out_shape=out_shape,
        compiler_params=_params("parallel"),
        name="ffn_ln",
    )(x2d, wg, wu, wd, g, b)


def _dil_proj_kernel(x_ref, w_ref, z_ref):
    z_ref[...] = _dot(x_ref[...], w_ref[...])


def _dil_proj(xb2d, w, l):
    m = xb2d.shape[0]
    return pl.pallas_call(
        _dil_proj_kernel,
        grid=(m // TM,),
        in_specs=[pl.BlockSpec((TM, D_MODEL), lambda i: (i, 0)), _layer_spec(l, D_MODEL, 3 * DIL_W)],
        out_specs=pl.BlockSpec((TM, 3 * DIL_W), lambda i: (i, 0)),
        out_shape=jax.ShapeDtypeStruct((m, 3 * DIL_W), F32),
        compiler_params=_params("parallel"),
        name="dil_proj",
    )(xb2d, w)


def _gla_kernel(x_ref, xn_ref, win_ref, wa_ref, ba_ref, gn_ref, wg_ref, bg_ref, wp_ref, o_ref, st_ref,
                za_ref, zb_ref):
    seq = x_ref.shape[0]
    rows, chunk = MIX_ROWS, GLA_CHUNK
    n_chunks = rows // chunk
    st_ref[...] = jnp.zeros(st_ref.shape, F32)

    causal = _chunk_causal(rows, chunk)
    tril = causal.astype(BF16)
    lane = lax.broadcasted_iota(jnp.int32, (1, LANES), 1)
    head_mask = (lane < GLA_DK, lane >= GLA_DK)

    def tile(i, z, _, side):
        sl = _tile_rows(i, rows)
        xa = _dot(z[:, G_SM:G_W].astype(BF16), wa_ref[...]) + ba_ref[...]
        la = _log_sigmoid(xa) * (1.0 / GLA_GATE_TEMP)
        side.step()
        bc = _chunk_cumsum(tril, la)
        bl = _chunk_last(bc, chunk)
        q = z[:, G_Q:G_K] * (GLA_DK ** -0.5)
        k = z[:, G_K:G_V]
        q_dec = (q * jnp.exp(bc))
        side.step()
        k_inv = (k * jnp.exp(-bc)).astype(BF16)
        k_dec = (k * jnp.exp(bl - bc)).astype(BF16)
        decay = jnp.exp(bl)
        side.step()
        heads, chunks = range(GLA_HEADS), range(n_chunks)
        pr = [slice((h // 2) * LANES, (h // 2 + 1) * LANES) for h in heads]
        csl = [slice(c * chunk, (c + 1) * chunk) for c in chunks]
        qd = [jnp.where(head_mask[h % 2], q_dec[:, pr[h]], 0.0).astype(BF16) for h in heads]
        vh = [z[:, G_V + h * GLA_DV:G_V + (h + 1) * GLA_DV].astype(BF16) for h in heads]
        att = [jnp.where(causal, _dot_nt(qd[h], k_inv[:, pr[h]]), 0.0).astype(BF16) for h in heads]
        side.step()
        o_intra = [_dot(att[h], vh[h]) for h in heads]
        kv_t = [[_dot_tn(vh[h][cs], k_dec[cs, pr[h]]) for cs in csl] for h in heads]
        side.step()
        s_st = [[st_ref[h]] for h in heads]
        for c in chunks:
            for h in heads:
                s_st[h].append(s_st[h][c] * decay[c * chunk:c * chunk + 1, pr[h]] + kv_t[h][c])
        for h in heads:
            st_ref[h] = s_st[h][n_chunks]
        o_inter = [[_dot_nt(qd[h][cs], s_st[h][c].astype(BF16)) for c, cs in enumerate(csl)] for h in heads]
        side.step()
        gate = jax.nn.sigmoid(_dot(x_ref[sl, :], wg_ref[...]) + bg_ref[...])
        ys = []
        for h in heads:
            hn = _head_norm(o_intra[h] + jnp.concatenate(o_inter[h], axis=0))
            hs = slice(h * GLA_DV, (h + 1) * GLA_DV)
            r = z[:, G_R + h * GLA_DV:G_R + (h + 1) * GLA_DV]
            ys.append((hn * gn_ref[:, hs] * (r * jax.nn.sigmoid(r))).astype(BF16))
            side.step()
        side.flush()
        y = jnp.concatenate(ys, axis=1)
        o_ref[sl, :] = (gate * _dot(y, wp_ref[...])).astype(o_ref.dtype)

    _tile_pairs(seq // rows, x_ref, xn_ref, win_ref, rows, tile, za_ref, zb_ref)


def _gla(xb, win, wa, ba, gn, wg, bg, wp, l):
    bsz, seq, _ = xb.shape
    xspec = pl.BlockSpec((None, seq, D_MODEL), lambda b: (b, 0, 0))
    return pl.pallas_call(
        _gla_kernel,
        grid=(bsz,),
        in_specs=[xspec, _next_tile_spec(bsz), _layer_spec(l, D_MODEL, G_W), _layer_spec(l, LANES, GLA_QK_W),
                  _layer_spec(l, 1, GLA_QK_W), _layer_spec(l, 1, GLA_V_W),
                  _layer_spec(l, D_MODEL, D_MODEL), _layer_spec(l, 1, D_MODEL),
                  _layer_spec(l, GLA_V_W, D_MODEL)],
        out_specs=xspec,
        out_shape=jax.ShapeDtypeStruct((bsz, seq, D_MODEL), BF16),
        scratch_shapes=[pltpu.VMEM((GLA_HEADS, GLA_DV, LANES), F32),
                        pltpu.VMEM((MIX_ROWS, G_W), F32), pltpu.VMEM((MIX_ROWS, G_W), F32)],
        compiler_params=_params("arbitrary"),
        name="gla",
    )(xb, xb, win, wa, ba, gn, wg, bg, wp)


def _dil_kernel(q_ref, k_ref, v_ref, bias_ref, o_ref, os_ref, ls_ref):
    seq = q_ref.shape[0]
    blk = DIL_BLOCK
    lane = lax.broadcasted_iota(jnp.int32, (1, LANES), 1)
    lo = lane < DIL_DH
    scale = DIL_DH ** -0.5

    def load(dil, start, pstart):
        qb = q_ref[pl.ds(start, blk, stride=dil), :] * scale
        q2 = jnp.concatenate([jnp.where(lo, qb, 0.0), jnp.where(lo, 0.0, qb)], axis=0).astype(BF16)
        kc = k_ref[pl.ds(start, blk, stride=dil), :]
        vc = v_ref[pl.ds(start, blk, stride=dil), :]
        if pstart is None:
            return q2, kc.astype(BF16), vc.astype(BF16)
        kp = k_ref[pl.ds(pstart, blk, stride=dil), :]
        vp = v_ref[pl.ds(pstart, blk, stride=dil), :]
        return (q2, jnp.concatenate([kp, kc], axis=0).astype(BF16),
                jnp.concatenate([vp, vc], axis=0).astype(BF16))

    def run(p, dil, total, block_of):
        u = max(d for d in range(1, DIL_UNROLL + 1) if total % d == 0)

        def step(g, carry):
            blocks = [block_of(g * u + j) for j in range(u)]
            ops = [load(dil, start, pstart) for start, pstart in blocks]
            n = range(u)
            s = [_dot_nt(ops[i][0], ops[i][1])
                 + (bias_ref[p] if blocks[i][1] is not None else bias_ref[p, :, blk:]) for i in n]
            m = [jnp.max(s[i], axis=-1, keepdims=True) for i in n]
            pe = [jnp.exp(s[i] - m[i]) for i in n]
            lsum = [jnp.sum(pe[i], axis=-1, keepdims=True) for i in n]
            o2 = [_dot(pe[i].astype(BF16), ops[i][2]) for i in n]
            num = [jnp.where(lo, o2[i][:blk], o2[i][blk:]) for i in n]
            den = [jnp.where(lo, lsum[i][:blk], lsum[i][blk:]) for i in n]
            mm = [jnp.where(lo, m[i][:blk], m[i][blk:]) for i in n]
            for i in n:
                start = blocks[i][0]
                os_ref[p, pl.ds(start, blk, stride=dil), :] = num[i] / den[i]
                ls_ref[p, pl.ds(start, blk, stride=dil), :] = mm[i] + jnp.log(den[i])
            return carry

        if total == u:
            step(0, 0)
        else:
            lax.fori_loop(0, total // u, step, 0)

    for p, (_, dil) in enumerate(DIL_PATTERNS):
        nb = seq // dil // blk
        span = dil * blk
        run(p, dil, dil, lambda r: (r, None))
        if nb > 1:
            def later(idx, nb=nb, span=span):
                start = idx // (nb - 1) + (1 + idx % (nb - 1)) * span
                return start, start - span

            run(p, dil, dil * (nb - 1), later)

    rows = MIX_ROWS

    def combine(i, carry):
        sl = pl.ds(pl.multiple_of(i * rows, rows), rows)
        l0, l1, l2 = ls_ref[0, sl, :], ls_ref[1, sl, :], ls_ref[2, sl, :]
        mx = jnp.maximum(jnp.maximum(l0, l1), l2)
        w0, w1, w2 = jnp.exp(l0 - mx), jnp.exp(l1 - mx), jnp.exp(l2 - mx)
        num = w0 * os_ref[0, sl, :] + w1 * os_ref[1, sl, :] + w2 * os_ref[2, sl, :]
        o_ref[sl, :] = (num / (w0 + w1 + w2)).astype(o_ref.dtype)
        return carry

    lax.fori_loop(0, seq // rows, combine, 0)


def _dil_bias_table():
    blk = DIL_BLOCK
    i = np.arange(blk)[:, None]
    j = np.arange(2 * blk)[None, :]
    dist = (i - j + blk).astype(np.float64)
    tab = np.zeros((DIL_HEADS // 2, len(DIL_PATTERNS), 2 * blk, 2 * blk), np.float32)
    for hp in range(DIL_HEADS // 2):
        for p, (window, dil) in enumerate(DIL_PATTERNS):
            valid = (dist >= 0) & (dist <= window // dil)
            for e in range(2):
                slope = 2.0 ** (-8.0 * (2 * hp + e + 1) / DIL_HEADS)
                tab[hp, p, e * blk:(e + 1) * blk] = np.where(valid, -slope * dist * dil, -np.inf)
    return jnp.asarray(tab)


def _dil(z, bias):
    bsz, seq, _ = z.shape
    npair = DIL_HEADS // 2

    def zspec(off):
        return pl.BlockSpec((None, seq, LANES), lambda b, hp: (b, 0, off // LANES + hp))

    return pl.pallas_call(
        _dil_kernel,
        grid=(bsz, npair),
        in_specs=[
            zspec(0), zspec(DIL_W), zspec(2 * DIL_W),
            pl.BlockSpec((None,) + bias.shape[1:], lambda b, hp: (hp, 0, 0, 0)),
        ],
        out_specs=pl.BlockSpec((None, seq, LANES), lambda b, hp: (b, 0, hp)),
        out_shape=jax.ShapeDtypeStruct((bsz, seq, DIL_W), BF16),
        scratch_shapes=[pltpu.VMEM((len(DIL_PATTERNS), seq, LANES), F32),
                        pltpu.VMEM((len(DIL_PATTERNS), seq, LANES), F32)],
        compiler_params=_params("parallel", "arbitrary"),
        name="dil_attn",
    )(z, z, z, bias)


def _mlstm_kernel(x_ref, xn_ref, win_ref, cw_ref, cb_ref, bi_ref, bf_ref, gn_ref, wg_ref, bg_ref, wp_ref,
                  o_ref, c_ref, n_ref, m_ref, za_ref, zb_ref):
    seq = x_ref.shape[0]
    rows, chunk = MIX_ROWS, ML_CHUNK
    n_chunks = rows // chunk
    c_ref[...] = jnp.zeros(c_ref.shape, F32)
    n_ref[...] = jnp.zeros(n_ref.shape, F32)
    m_ref[...] = jnp.zeros(m_ref.shape, F32)
    zb_ref[rows:, :] = jnp.zeros((SUBLANES, M_W), F32)

    causal = _chunk_causal(rows, chunk)
    tril = causal.astype(BF16)
    pad = SUBLANES

    def tile(i, z, z_other, side):
        sl = _tile_rows(i, rows)
        z[0:pad, M_QK:M_V] = jnp.where(i > 0, z_other[rows:, M_QK:M_V], 0.0)
        conv = cb_ref[...]
        for tap in range(ML_CONV):
            off = pad - (ML_CONV - 1) + tap
            conv = conv + z[off:off + rows, M_QK:M_V] * cw_ref[tap:tap + 1, :]
        qk = conv * jax.nn.sigmoid(conv)
        side.step()

        sm = z[pad:, M_SM:M_W]
        logf = _log_sigmoid(sm + bf_ref[...])
        logi = sm + bi_ref[...]
        f_cum = _chunk_cumsum(tril, logf)
        f_tot = _chunk_last(f_cum, chunk)
        f_cum_t = f_cum.T
        logi_t = logi.T
        side.step()

        heads, chunks = range(ML_HEADS), range(n_chunks)
        hsl = [slice(h * ML_DH, (h + 1) * ML_DH) for h in heads]
        csl = [slice(c * chunk, (c + 1) * chunk) for c in chunks]
        q = [qk[:, hsl[h]] for h in heads]
        k = [qk[:, ML_W + h * ML_DH:ML_W + (h + 1) * ML_DH] * (ML_DH ** -0.5) for h in heads]
        v = [z[pad:, M_V + h * ML_DH:M_V + (h + 1) * ML_DH] for h in heads]
        qb = [x.astype(BF16) for x in q]
        kb = [x.astype(BF16) for x in k]
        vb = [x.astype(BF16) for x in v]
        fc = [f_cum[:, SM_F + h:SM_F + h + 1] for h in heads]
        ft = [f_tot[:, SM_F + h:SM_F + h + 1] for h in heads]
        li = [logi[:, SM_I + h:SM_I + h + 1] for h in heads]
        fr = [f_cum_t[SM_F + h:SM_F + h + 1, :] for h in heads]
        lir = [logi_t[SM_I + h:SM_I + h + 1, :] for h in heads]
        dlog = [jnp.where(causal, fc[h] - fr[h] + lir[h], -jnp.inf) for h in heads]
        m_intra = [jnp.max(dlog[h], axis=-1, keepdims=True) for h in heads]
        side.step()
        qkt = [_dot_nt(qb[h], kb[h]) for h in heads]
        sqk = [qkt[h] * jnp.exp(dlog[h] - m_intra[h]) for h in heads]
        side.step()
        num_intra = [_dot(sqk[h].astype(BF16), vb[h]) for h in heads]
        den_intra = [jnp.sum(sqk[h], axis=-1, keepdims=True) for h in heads]
        side.step()
        g = [ft[h] - fc[h] + li[h] for h in heads]
        m_chunk = [[jnp.max(g[h][cs], axis=0, keepdims=True) for cs in csl] for h in heads]
        wgt = [[jnp.exp(g[h][cs] - m_chunk[h][c]) for c, cs in enumerate(csl)] for h in heads]
        kv = [[_dot_tn((v[h][cs] * wgt[h][c]).astype(BF16), kb[h][cs]) for c, cs in enumerate(csl)]
              for h in heads]
        nk = [[jnp.sum(k[h][cs] * wgt[h][c], axis=0, keepdims=True) for c, cs in enumerate(csl)]
              for h in heads]
        f_last = [[ft[h][c * chunk:c * chunk + 1] for c in chunks] for h in heads]
        side.step()
        c_st = [[c_ref[h]] for h in heads]
        n_st = [[n_ref[h]] for h in heads]
        m_st = [[m_ref[h]] for h in heads]
        for c in chunks:
            for h in heads:
                m_prev = m_st[h][c]
                m_new = jnp.maximum(f_last[h][c] + m_prev, m_chunk[h][c])
                a = jnp.exp(f_last[h][c] + m_prev - m_new)
                b = jnp.exp(m_chunk[h][c] - m_new)
                c_st[h].append(a * c_st[h][c] + b * kv[h][c])
                n_st[h].append(a * n_st[h][c] + b * nk[h][c])
                m_st[h].append(m_new)
        for h in heads:
            c_ref[h], n_ref[h], m_ref[h] = c_st[h][n_chunks], n_st[h][n_chunks], m_st[h][n_chunks]
        side.step()
        inter = [[_dot_nt(qb[h][cs], c_st[h][c].astype(BF16)) for c, cs in enumerate(csl)] for h in heads]
        qn = [[jnp.sum(q[h][cs] * n_st[h][c], axis=-1, keepdims=True) for c, cs in enumerate(csl)]
              for h in heads]
        side.step()
        gate = jax.nn.sigmoid(_dot(x_ref[sl, :], wg_ref[...]) + bg_ref[...])
        ys = []
        for h in heads:
            side.step()
            outs = []
            for c, cs in enumerate(csl):
                a_t = fc[h][cs] + m_st[h][c]
                m_t = jnp.maximum(a_t, m_intra[h][cs])
                w_inter = jnp.exp(a_t - m_t)
                w_intra = jnp.exp(m_intra[h][cs] - m_t)
                num = w_inter * inter[h][c] + w_intra * num_intra[h][cs]
                den = w_inter * qn[h][c] + w_intra * den_intra[h][cs]
                outs.append(num / jnp.maximum(jnp.abs(den), jnp.exp(-m_t)))
            hn = _head_norm(jnp.concatenate(outs, axis=0))
            og = z[pad:, M_O + h * ML_DH:M_O + (h + 1) * ML_DH]
            ys.append((hn * gn_ref[:, hsl[h]] * jax.nn.sigmoid(og)).astype(BF16))
        side.flush()
        y = jnp.concatenate(ys, axis=1)
        o_ref[sl, :] = (gate * _dot(y, wp_ref[...])).astype(o_ref.dtype)

    _tile_pairs(seq // rows, x_ref, xn_ref, win_ref, rows, tile, za_ref, zb_ref)


def _mlstm(xb, win, cw, cb, bi, bf, gn, wg, bg, wp, l):
    bsz, seq, _ = xb.shape
    xspec = pl.BlockSpec((None, seq, D_MODEL), lambda b: (b, 0, 0))
    return pl.pallas_call(
        _mlstm_kernel,
        grid=(bsz,),
        in_specs=[xspec, _next_tile_spec(bsz), _layer_spec(l, D_MODEL, M_W), _layer_spec(l, ML_CONV, 2 * ML_W),
                  _layer_spec(l, 1, 2 * ML_W), _layer_spec(l, 1, LANES), _layer_spec(l, 1, LANES),
                  _layer_spec(l, 1, ML_W), _layer_spec(l, D_MODEL, D_MODEL), _layer_spec(l, 1, D_MODEL),
                  _layer_spec(l, ML_W, D_MODEL)],
        out_specs=xspec,
        out_shape=jax.ShapeDtypeStruct((bsz, seq, D_MODEL), BF16),
        scratch_shapes=[pltpu.VMEM((ML_HEADS, ML_DH, ML_DH), F32),
                        pltpu.VMEM((ML_HEADS, 1, ML_DH), F32),
                        pltpu.VMEM((ML_HEADS, 1, 1), F32),
                        pltpu.VMEM((SUBLANES + MIX_ROWS, M_W), F32),
                        pltpu.VMEM((SUBLANES + MIX_ROWS, M_W), F32)],
        compiler_params=_params("arbitrary"),
        name="mlstm",
    )(xb, xb, win, cw, cb, bi, bf, gn, wg, bg, wp)


def _merge_ln_kernel(x_ref, xb_ref, gg_ref, gm_ref, yd_ref, wg_ref, bg_ref, wpd_ref, wo_ref, g_ref, b_ref,
                     o_ref):
    gate_d = jax.nn.sigmoid(_dot(xb_ref[...], wg_ref[...]) + bg_ref[...])
    merged = gg_ref[...].astype(F32) + gm_ref[...].astype(F32) + gate_d * _dot(yd_ref[...], wpd_ref[...])
    mix = _dot(merged.astype(BF16), wo_ref[...])
    o_ref[...] = _ln(DN_ALPHA * x_ref[...] + mix, g_ref[...], b_ref[...])


def _merge_ln(x2d, xb2d, gg, gm, yd, wg, bg, wpd, wo, g, b, l):
    m = x2d.shape[0]

    def row(width):
        return pl.BlockSpec((TM, width), lambda i: (i, 0))

    return pl.pallas_call(
        _merge_ln_kernel,
        grid=(m // TM,),
        in_specs=[row(D_MODEL), row(D_MODEL), row(D_MODEL), row(D_MODEL), row(DIL_W),
                  _layer_spec(l, D_MODEL, D_MODEL), _layer_spec(l, 1, D_MODEL),
                  _layer_spec(l, DIL_W, D_MODEL), _layer_spec(l, D_MODEL, D_MODEL),
                  _layer_spec(l, 1, D_MODEL), _layer_spec(l, 1, D_MODEL)],
        out_specs=row(D_MODEL),
        out_shape=jax.ShapeDtypeStruct((m, D_MODEL), F32),
        compiler_params=_params("parallel"),
        name="merge_ln",
    )(x2d, xb2d, gg, gm, yd, wg, bg, wpd, wo, g, b)


def _stage_w_in(w_in):
    w_in = w_in.astype(BF16)

    def cols(idx):
        return w_in[:, :, IN_OFFS[idx]:IN_OFFS[idx + 1]]

    depth = w_in.shape[0]
    small = jnp.concatenate(
        [cols(4), cols(11), cols(12),
         jnp.zeros((depth, D_MODEL, LANES - GLA_RANK - 2 * ML_HEADS), w_in.dtype)], axis=-1)
    w_gla = jnp.concatenate([cols(0), cols(1), cols(2), cols(3), small], axis=-1)
    w_dil = jnp.concatenate([cols(5), cols(6), cols(7)], axis=-1)
    w_ml = jnp.concatenate([cols(8), cols(9), cols(10), cols(13), small], axis=-1)
    gates = cols(14)
    d = D_MODEL
    return w_gla, w_dil, w_ml, gates[:, :, 0:d], gates[:, :, d:2 * d], gates[:, :, 2 * d:3 * d]


def _lane_vec(vals, lane0):
    depth, n = vals.shape
    out = jnp.zeros((depth, 1, LANES), F32)
    return out.at[:, 0, lane0:lane0 + n].set(vals.astype(F32))


def kernel(x, ffn1_w_gate, ffn1_w_up, ffn1_w_down, ln1_g, ln1_b, w_in, gla_w_a2, gla_b_a, gla_norm_g,
           ml_conv_w, ml_conv_b, ml_b_i, ml_b_f, ml_norm_g, w_proj_gla, w_proj_dil, w_proj_ml, b_gate,
           w_out, ln2_g, ln2_b, ffn2_w_gate, ffn2_w_up, ffn2_w_down, ln3_g, ln3_b):
    bsz, seq, d = x.shape
    depth = w_in.shape[0]
    assert d == D_MODEL and (bsz * seq) % TM == 0 and seq % (16 * DIL_BLOCK) == 0

    bf = lambda w: w.astype(BF16)
    row = lambda v: v.astype(F32)[:, None, :]
    f1g, f1u, f1d = bf(ffn1_w_gate), bf(ffn1_w_up), bf(ffn1_w_down)
    f2g, f2u, f2d = bf(ffn2_w_gate), bf(ffn2_w_up), bf(ffn2_w_down)
    w_gla, w_dil, w_ml, wg_gla, wg_dil, wg_ml = _stage_w_in(w_in)
    bg = b_gate.astype(F32)
    bg_gla, bg_dil, bg_ml = row(bg[:, 0:d]), row(bg[:, d:2 * d]), row(bg[:, 2 * d:3 * d])
    wa = jnp.zeros((depth, LANES, GLA_QK_W), BF16).at[:, SM_A:SM_A + GLA_RANK, :].set(bf(gla_w_a2))
    bi_vec, bf_vec = _lane_vec(ml_b_i, SM_I), _lane_vec(ml_b_f, SM_F)
    wpg, wpd, wpm, wo = bf(w_proj_gla), bf(w_proj_dil), bf(w_proj_ml), bf(w_out)
    bias = _dil_bias_table()

    h = x.reshape(bsz * seq, d)
    for l in range(depth):
        h, hb = _ffn_ln(h, f1g, f1u, f1d, row(ln1_g), row(ln1_b), l, True)
        hb3 = hb.reshape(bsz, seq, d)
        g_gla = _gla(hb3, w_gla, wa, row(gla_b_a), row(gla_norm_g), wg_gla, bg_gla, wpg, l)
        z_dil = _dil_proj(hb, w_dil, l).reshape(bsz, seq, 3 * DIL_W)
        y_dil = _dil(z_dil, bias)
        g_ml = _mlstm(hb3, w_ml, ml_conv_w.astype(F32), row(ml_conv_b), bi_vec, bf_vec, row(ml_norm_g),
                      wg_ml, bg_ml, wpm, l)
        flat = lambda y: y.reshape(bsz * seq, y.shape[-1])
        h = _merge_ln(h, hb, flat(g_gla), flat(g_ml), flat(y_dil), wg_dil, bg_dil, wpd, wo,
                      row(ln2_g), row(ln2_b), l)
        (h,) = _ffn_ln(h, f2g, f2u, f2d, row(ln3_g), row(ln3_b), l, False)
    return h.reshape(bsz, seq, d)
```

```python
import jax
import jax.numpy as jnp
import numpy as np
from jax import lax
from jax.experimental import pallas as pl
from jax.experimental.pallas import tpu as pltpu

F32 = jnp.float32
BF16 = jnp.bfloat16

D_MODEL = 1024
DEPTH = 4
GLA_HEADS, GLA_DK, GLA_DV, GLA_RANK, GLA_CHUNK = 4, 64, 128, 16, 64
GLA_GATE_TEMP = 16.0
DIL_HEADS, DIL_DH, DIL_BLOCK = 8, 64, 128
DIL_PATTERNS = ((128, 1), (512, 4), (2048, 16))
ML_HEADS, ML_DH, ML_CHUNK, ML_CONV = 4, 128, 64, 4
D_FF = 2816
DN_ALPHA = (2 * DEPTH) ** 0.25
LN_EPS = 1e-5

GLA_QK_W = GLA_HEADS * GLA_DK
GLA_V_W = GLA_HEADS * GLA_DV
DIL_W = DIL_HEADS * DIL_DH
ML_W = ML_HEADS * ML_DH
IN_WIDTHS = (GLA_QK_W, GLA_QK_W, GLA_V_W, GLA_V_W, GLA_RANK, DIL_W, DIL_W, DIL_W,
             ML_W, ML_W, ML_W, ML_HEADS, ML_HEADS, ML_W, 3 * D_MODEL)
IN_OFFS = tuple(int(v) for v in np.cumsum((0,) + IN_WIDTHS))

LANES = 128
SUBLANES = 8
VMEM_LIMIT = 56 * 1024 * 1024
TM = 512
DIL_UNROLL = 16
MIX_ROWS = 256

G_Q, G_K, G_V, G_R, G_SM, G_W = 0, 256, 512, 1024, 1536, 1664
M_QK, M_V, M_O, M_SM, M_W = 0, 1024, 1536, 2048, 2176
SM_A, SM_I, SM_F = 0, 16, 20


def _ln(r, g, b):
    mu = jnp.mean(r, axis=-1, keepdims=True)
    d = r - mu
    var = jnp.mean(d * d, axis=-1, keepdims=True)
    return d * lax.rsqrt(var + LN_EPS) * g + b


def _head_norm(o):
    mu = jnp.mean(o, axis=-1, keepdims=True)
    d = o - mu
    var = jnp.mean(d * d, axis=-1, keepdims=True)
    return d * lax.rsqrt(var + LN_EPS)


def _log_sigmoid(x):
    return -(jnp.maximum(-x, 0.0) + jnp.log1p(jnp.exp(-jnp.abs(x))))


def _dot(a, b):
    return jnp.dot(a, b, preferred_element_type=F32)


def _dot_nt(a, b):
    return lax.dot_general(a, b, (((1,), (1,)), ((), ())), preferred_element_type=F32)


def _dot_tn(a, b):
    return lax.dot_general(a, b, (((0,), (0,)), ((), ())), preferred_element_type=F32)


def _chunk_cumsum(tril, x):
    hi = x.astype(BF16)
    r1 = x - hi.astype(F32)
    mid = r1.astype(BF16)
    lo = (r1 - mid.astype(F32)).astype(BF16)
    return _dot(tril, hi) + _dot(tril, mid) + _dot(tril, lo)


def _chunk_last(x, chunk):
    rows, w = x.shape
    return jnp.concatenate(
        [jnp.broadcast_to(x[c * chunk + chunk - 1:(c + 1) * chunk, :], (chunk, w)) for c in range(rows // chunk)],
        axis=0)


PROJ_COLS = 512


class _SideWork:
    def __init__(self, jobs):
        self._jobs = list(jobs)

    def step(self, n=1):
        for _ in range(n):
            if self._jobs:
                self._jobs.pop(0)()

    def flush(self):
        self.step(len(self._jobs))


def _tile_rows(i, rows):
    return pl.ds(i * rows if isinstance(i, int) else pl.multiple_of(i * rows, rows), rows)


def _proj_jobs(x_ref, w_ref, rows, i, z):
    sl = _tile_rows(i, rows)
    row0, width = z.shape[0] - rows, z.shape[1]

    def job(c0):
        c1 = min(c0 + PROJ_COLS, width)
        z[row0:, c0:c1] = _dot(x_ref[sl, :], w_ref[:, c0:c1])

    return [lambda c0=c0: job(c0) for c0 in range(0, width, PROJ_COLS)]


def _tile_pairs(n_tiles, x_ref, xn_ref, w_ref, rows, tile, za, zb):
    assert n_tiles % 2 == 0

    @pl.when(pl.program_id(0) == 0)
    def _():
        _SideWork(_proj_jobs(x_ref, w_ref, rows, 0, za)).flush()

    def pair(a, next_jobs):
        tile(a, za, zb, _SideWork(_proj_jobs(x_ref, w_ref, rows, a + 1, zb)))
        tile(a + 1, zb, za, _SideWork(next_jobs))

    def body(j, carry):
        pair(2 * j, _proj_jobs(x_ref, w_ref, rows, 2 * j + 2, za))
        return carry

    lax.fori_loop(0, n_tiles // 2 - 1, body, 0)
    pair(n_tiles - 2, _proj_jobs(xn_ref, w_ref, rows, 0, za))


def _chunk_causal(rows, chunk):
    r = lax.broadcasted_iota(jnp.int32, (rows, rows), 0)
    c = lax.broadcasted_iota(jnp.int32, (rows, rows), 1)
    return ((r // chunk) == (c // chunk)) & (c <= r)


def _resident(shape, index_map):
    return pl.BlockSpec(shape, index_map, pipeline_mode=pl.Buffered(1))


def _layer_spec(l, r, c):
    return _resident((None, r, c), lambda *_: (l, 0, 0))


def _next_tile_spec(bsz):
    return pl.BlockSpec((None, MIX_ROWS, D_MODEL), lambda b: (jnp.minimum(b + 1, bsz - 1), 0, 0))


def _params(*sem):
    return pltpu.CompilerParams(dimension_semantics=sem, vmem_limit_bytes=VMEM_LIMIT)


def _ffn_ln_kernel(x_ref, wg_ref, wu_ref, wd_ref, g_ref, b_ref, o_ref, *ob_ref):
    half = x_ref.shape[0] // 2
    sls = [slice(0, half), slice(half, 2 * half)]
    x = [x_ref[s, :] for s in sls]
    xb = [v.astype(BF16) for v in x]
    g = [_dot(v, wg_ref[...]) for v in xb]
    u = [_dot(v, wu_ref[...]) for v in xb]
    h = [(g[i] * jax.nn.sigmoid(g[i]) * u[i]).astype(BF16) for i in range(2)]
    y = [_dot(v, wd_ref[...]) for v in h]
    for i, s in enumerate(sls):
        out = _ln(DN_ALPHA * x[i] + 0.5 * y[i], g_ref[...], b_ref[...])
        o_ref[s, :] = out
        for ref in ob_ref:
            ref[s, :] = out.astype(BF16)


def _ffn_ln(x2d, wg, wu, wd, g, b, l, with_bf16):
    m = x2d.shape[0]
    row = pl.BlockSpec((TM, D_MODEL), lambda i: (i, 0))
    out_shape = [jax.ShapeDtypeStruct((m, D_MODEL), F32)]
    if with_bf16:
        out_shape.append(jax.ShapeDtypeStruct((m, D_MODEL), BF16))
    return pl.pallas_call(
        _ffn_ln_kernel,
        grid=(m // TM,),
        in_specs=[row, _layer_spec(l, D_MODEL, D_FF), _layer_spec(l, D_MODEL, D_FF),
                  _layer_spec(l, D_FF, D_MODEL), _layer_spec(l, 1, D_MODEL), _layer_spec(l, 1, D_MODEL)],
        out_specs=[row] * len(out_shape),
        out_shape=out_shape,
        compiler_params=_params("parallel"),
        name="ffn_ln",
    )(x2d, wg, wu, wd, g, b)


def _dil_proj_kernel(x_ref, w_ref, z_ref):
    z_ref[...] = _dot(x_ref[...], w_ref[...])


def _dil_proj(xb2d, w, l):
    m = xb2d.shape[0]
    return pl.pallas_call(
        _dil_proj_kernel,
        grid=(m // TM,),
        in_specs=[pl.BlockSpec((TM, D_MODEL), lambda i: (i, 0)), _layer_spec(l, D_MODEL, 3 * DIL_W)],
        out_specs=pl.BlockSpec((TM, 3 * DIL_W), lambda i: (i, 0)),
        out_shape=jax.ShapeDtypeStruct((m, 3 * DIL_W), F32),
        compiler_params=_params("parallel"),
        name="dil_proj",
    )(xb2d, w)


def _gla_kernel(x_ref, xn_ref, win_ref, wa_ref, ba_ref, gn_ref, wg_ref, bg_ref, wp_ref, o_ref, st_ref,
                za_ref, zb_ref):
    seq = x_ref.shape[0]
    rows, chunk = MIX_ROWS, GLA_CHUNK
    n_chunks = rows // chunk
    st_ref[...] = jnp.zeros(st_ref.shape, F32)

    causal = _chunk_causal(rows, chunk)
    tril = causal.astype(BF16)
    lane = lax.broadcasted_iota(jnp.int32, (1, LANES), 1)
    head_mask = (lane < GLA_DK, lane >= GLA_DK)

    def tile(i, z, _, side):
        sl = _tile_rows(i, rows)
        xa = _dot(z[:, G_SM:G_W].astype(BF16), wa_ref[...]) + ba_ref[...]
        la = _log_sigmoid(xa) * (1.0 / GLA_GATE_TEMP)
        side.step()
        bc = _chunk_cumsum(tril, la)
        bl = _chunk_last(bc, chunk)
        q = z[:, G_Q:G_K] * (GLA_DK ** -0.5)
        k = z[:, G_K:G_V]
        q_dec = (q * jnp.exp(bc))
        side.step()
        k_inv = (k * jnp.exp(-bc)).astype(BF16)
        k_dec = (k * jnp.exp(bl - bc)).astype(BF16)
        decay = jnp.exp(bl)
        side.step()
        heads, chunks = range(GLA_HEADS), range(n_chunks)
        pr = [slice((h // 2) * LANES, (h // 2 + 1) * LANES) for h in heads]
        csl = [slice(c * chunk, (c + 1) * chunk) for c in chunks]
        qd = [jnp.where(head_mask[h % 2], q_dec[:, pr[h]], 0.0).astype(BF16) for h in heads]
        vh = [z[:, G_V + h * GLA_DV:G_V + (h + 1) * GLA_DV].astype(BF16) for h in heads]
        att = [jnp.where(causal, _dot_nt(qd[h], k_inv[:, pr[h]]), 0.0).astype(BF16) for h in heads]
        side.step()
        o_intra = [_dot(att[h], vh[h]) for h in heads]
        kv_t = [[_dot_tn(vh[h][cs], k_dec[cs, pr[h]]) for cs in csl] for h in heads]
        side.step()
        s_st = [[st_ref[h]] for h in heads]
        for c in chunks:
            for h in heads:
                s_st[h].append(s_st[h][c] * decay[c * chunk:c * chunk + 1, pr[h]] + kv_t[h][c])
        for h in heads:
            st_ref[h] = s_st[h][n_chunks]
        o_inter = [[_dot_nt(qd[h][cs], s_st[h][c].astype(BF16)) for c, cs in enumerate(csl)] for h in heads]
        side.step()
        gate = jax.nn.sigmoid(_dot(x_ref[sl, :], wg_ref[...]) + bg_ref[...])
        ys = []
        for h in heads:
            hn = _head_norm(o_intra[h] + jnp.concatenate(o_inter[h], axis=0))
            hs = slice(h * GLA_DV, (h + 1) * GLA_DV)
            r = z[:, G_R + h * GLA_DV:G_R + (h + 1) * GLA_DV]
            ys.append((hn * gn_ref[:, hs] * (r * jax.nn.sigmoid(r))).astype(BF16))
            side.step()
        side.flush()
        y = jnp.concatenate(ys, axis=1)
        o_ref[sl, :] = (gate * _dot(y, wp_ref[...])).astype(o_ref.dtype)

    _tile_pairs(seq // rows, x_ref, xn_ref, win_ref, rows, tile, za_ref, zb_ref)


def _gla(xb, win, wa, ba, gn, wg, bg, wp, l):
    bsz, seq, _ = xb.shape
    xspec = pl.BlockSpec((None, seq, D_MODEL), lambda b: (b, 0, 0))
    return pl.pallas_call(
        _gla_kernel,
        grid=(bsz,),
        in_specs=[xspec, _next_tile_spec(bsz), _layer_spec(l, D_MODEL, G_W), _layer_spec(l, LANES, GLA_QK_W),
                  _layer_spec(l, 1, GLA_QK_W), _layer_spec(l, 1, GLA_V_W),
                  _layer_spec(l, D_MODEL, D_MODEL), _layer_spec(l, 1, D_MODEL),
                  _layer_spec(l, GLA_V_W, D_MODEL)],
        out_specs=xspec,
        out_shape=jax.ShapeDtypeStruct((bsz, seq, D_MODEL), BF16),
        scratch_shapes=[pltpu.VMEM((GLA_HEADS, GLA_DV, LANES), F32),
                        pltpu.VMEM((MIX_ROWS, G_W), F32), pltpu.VMEM((MIX_ROWS, G_W), F32)],
        compiler_params=_params("arbitrary"),
        name="gla",
    )(xb, xb, win, wa, ba, gn, wg, bg, wp)


def _dil_kernel(q_ref, k_ref, v_ref, bias_ref, o_ref, os_ref, ls_ref):
    seq = q_ref.shape[0]
    blk = DIL_BLOCK
    lane = lax.broadcasted_iota(jnp.int32, (1, LANES), 1)
    lo = lane < DIL_DH
    scale = DIL_DH ** -0.5

    def load(dil, start, pstart):
        qb = q_ref[pl.ds(start, blk, stride=dil), :] * scale
        q2 = jnp.concatenate([jnp.where(lo, qb, 0.0), jnp.where(lo, 0.0, qb)], axis=0).astype(BF16)
        kc = k_ref[pl.ds(start, blk, stride=dil), :]
        vc = v_ref[pl.ds(start, blk, stride=dil), :]
        if pstart is None:
            return q2, kc.astype(BF16), vc.astype(BF16)
        kp = k_ref[pl.ds(pstart, blk, stride=dil), :]
        vp = v_ref[pl.ds(pstart, blk, stride=dil), :]
        return (q2, jnp.concatenate([kp, kc], axis=0).astype(BF16),
                jnp.concatenate([vp, vc], axis=0).astype(BF16))

    def run(p, dil, total, block_of):
        u = max(d for d in range(1, DIL_UNROLL + 1) if total % d == 0)

        def step(g, carry):
            blocks = [block_of(g * u + j) for j in range(u)]
            ops = [load(dil, start, pstart) for start, pstart in blocks]
            n = range(u)
            s = [_dot_nt(ops[i][0], ops[i][1])
                 + (bias_ref[p] if blocks[i][1] is not None else bias_ref[p, :, blk:]) for i in n]
            m = [jnp.max(s[i], axis=-1, keepdims=True) for i in n]
            pe = [jnp.exp(s[i] - m[i]) for i in n]
            lsum = [jnp.sum(pe[i], axis=-1, keepdims=True) for i in n]
            o2 = [_dot(pe[i].astype(BF16), ops[i][2]) for i in n]
            num = [jnp.where(lo, o2[i][:blk], o2[i][blk:]) for i in n]
            den = [jnp.where(lo, lsum[i][:blk], lsum[i][blk:]) for i in n]
            mm = [jnp.where(lo, m[i][:blk], m[i][blk:]) for i in n]
            for i in n:
                start = blocks[i][0]
                os_ref[p, pl.ds(start, blk, stride=dil), :] = num[i] / den[i]
                ls_ref[p, pl.ds(start, blk, stride=dil), :] = mm[i] + jnp.log(den[i])
            return carry

        if total == u:
            step(0, 0)
        else:
            lax.fori_loop(0, total // u, step, 0)

    for p, (_, dil) in enumerate(DIL_PATTERNS):
        nb = seq // dil // blk
        span = dil * blk
        run(p, dil, dil, lambda r: (r, None))
        if nb > 1:
            def later(idx, nb=nb, span=span):
                start = idx // (nb - 1) + (1 + idx % (nb - 1)) * span
                return start, start - span

            run(p, dil, dil * (nb - 1), later)

    rows = MIX_ROWS

    def combine(i, carry):
        sl = pl.ds(pl.multiple_of(i * rows, rows), rows)
        l0, l1, l2 = ls_ref[0, sl, :], ls_ref[1, sl, :], ls_ref[2, sl, :]
        mx = jnp.maximum(jnp.maximum(l0, l1), l2)
        w0, w1, w2 = jnp.exp(l0 - mx), jnp.exp(l1 - mx), jnp.exp(l2 - mx)
        num = w0 * os_ref[0, sl, :] + w1 * os_ref[1, sl, :] + w2 * os_ref[2, sl, :]
        o_ref[sl, :] = (num / (w0 + w1 + w2)).astype(o_ref.dtype)
        return carry

    lax.fori_loop(0, seq // rows, combine, 0)


def _dil_bias_table():
    blk = DIL_BLOCK
    i = np.arange(blk)[:, None]
    j = np.arange(2 * blk)[None, :]
    dist = (i - j + blk).astype(np.float64)
    tab = np.zeros((DIL_HEADS // 2, len(DIL_PATTERNS), 2 * blk, 2 * blk), np.float32)
    for hp in range(DIL_HEADS // 2):
        for p, (window, dil) in enumerate(DIL_PATTERNS):
            valid = (dist >= 0) & (dist <= window // dil)
            for e in range(2):
                slope = 2.0 ** (-8.0 * (2 * hp + e + 1) / DIL_HEADS)
                tab[hp, p, e * blk:(e + 1) * blk] = np.where(valid, -slope * dist * dil, -np.inf)
    return jnp.asarray(tab)


def _dil(z, bias):
    bsz, seq, _ = z.shape
    npair = DIL_HEADS // 2

    def zspec(off):
        return pl.BlockSpec((None, seq, LANES), lambda b, hp: (b, 0, off // LANES + hp))

    return pl.pallas_call(
        _dil_kernel,
        grid=(bsz, npair),
        in_specs=[
            zspec(0), zspec(DIL_W), zspec(2 * DIL_W),
            pl.BlockSpec((None,) + bias.shape[1:], lambda b, hp: (hp, 0, 0, 0)),
        ],
        out_specs=pl.BlockSpec((None, seq, LANES), lambda b, hp: (b, 0, hp)),
        out_shape=jax.ShapeDtypeStruct((bsz, seq, DIL_W), BF16),
        scratch_shapes=[pltpu.VMEM((len(DIL_PATTERNS), seq, LANES), F32),
                        pltpu.VMEM((len(DIL_PATTERNS), seq, LANES), F32)],
        compiler_params=_params("parallel", "arbitrary"),
        name="dil_attn",
    )(z, z, z, bias)


def _mlstm_kernel(x_ref, xn_ref, win_ref, cw_ref, cb_ref, bi_ref, bf_ref, gn_ref, wg_ref, bg_ref, wp_ref,
                  o_ref, c_ref, n_ref, m_ref, za_ref, zb_ref):
    seq = x_ref.shape[0]
    rows, chunk = MIX_ROWS, ML_CHUNK
    n_chunks = rows // chunk
    c_ref[...] = jnp.zeros(c_ref.shape, F32)
    n_ref[...] = jnp.zeros(n_ref.shape, F32)
    m_ref[...] = jnp.zeros(m_ref.shape, F32)
    zb_ref[rows:, :] = jnp.zeros((SUBLANES, M_W), F32)

    causal = _chunk_causal(rows, chunk)
    tril = causal.astype(BF16)
    pad = SUBLANES

    def tile(i, z, z_other, side):
        sl = _tile_rows(i, rows)
        z[0:pad, M_QK:M_V] = jnp.where(i > 0, z_other[rows:, M_QK:M_V], 0.0)
        conv = cb_ref[...]
        for tap in range(ML_CONV):
            off = pad - (ML_CONV - 1) + tap
            conv = conv + z[off:off + rows, M_QK:M_V] * cw_ref[tap:tap + 1, :]
        qk = conv * jax.nn.sigmoid(conv)
        side.step()

        sm = z[pad:, M_SM:M_W]
        logf = _log_sigmoid(sm + bf_ref[...])
        logi = sm + bi_ref[...]
        f_cum = _chunk_cumsum(tril, logf)
        f_tot = _chunk_last(f_cum, chunk)
        f_cum_t = f_cum.T
        logi_t = logi.T
        side.step()

        heads, chunks = range(ML_HEADS), range(n_chunks)
        hsl = [slice(h * ML_DH, (h + 1) * ML_DH) for h in heads]
        csl = [slice(c * chunk, (c + 1) * chunk) for c in chunks]
        q = [qk[:, hsl[h]] for h in heads]
        k = [qk[:, ML_W + h * ML_DH:ML_W + (h + 1) * ML_DH] * (ML_DH ** -0.5) for h in heads]
        v = [z[pad:, M_V + h * ML_DH:M_V + (h + 1) * ML_DH] for h in heads]
        qb = [x.astype(BF16) for x in q]
        kb = [x.astype(BF16) for x in k]
        vb = [x.astype(BF16) for x in v]
        fc = [f_cum[:, SM_F + h:SM_F + h + 1] for h in heads]
        ft = [f_tot[:, SM_F + h:SM_F + h + 1] for h in heads]
        li = [logi[:, SM_I + h:SM_I + h + 1] for h in heads]
        fr = [f_cum_t[SM_F + h:SM_F + h + 1, :] for h in heads]
        lir = [logi_t[SM_I + h:SM_I + h + 1, :] for h in heads]
        dlog = [jnp.where(causal, fc[h] - fr[h] + lir[h], -jnp.inf) for h in heads]
        m_intra = [jnp.max(dlog[h], axis=-1, keepdims=True) for h in heads]
        side.step()
        qkt = [_dot_nt(qb[h], kb[h]) for h in heads]
        sqk = [qkt[h] * jnp.exp(dlog[h] - m_intra[h]) for h in heads]
        side.step()
        num_intra = [_dot(sqk[h].astype(BF16), vb[h]) for h in heads]
        den_intra = [jnp.sum(sqk[h], axis=-1, keepdims=True) for h in heads]
        side.step()
        g = [ft[h] - fc[h] + li[h] for h in heads]
        m_chunk = [[jnp.max(g[h][cs], axis=0, keepdims=True) for cs in csl] for h in heads]
        wgt = [[jnp.exp(g[h][cs] - m_chunk[h][c]) for c, cs in enumerate(csl)] for h in heads]
        kv = [[_dot_tn((v[h][cs] * wgt[h][c]).astype(BF16), kb[h][cs]) for c, cs in enumerate(csl)]
              for h in heads]
        nk = [[jnp.sum(k[h][cs] * wgt[h][c], axis=0, keepdims=True) for c, cs in enumerate(csl)]
              for h in heads]
        f_last = [[ft[h][c * chunk:c * chunk + 1] for c in chunks] for h in heads]
        side.step()
        c_st = [[c_ref[h]] for h in heads]
        n_st = [[n_ref[h]] for h in heads]
        m_st = [[m_ref[h]] for h in heads]
        for c in chunks:
            for h in heads:
                m_prev = m_st[h][c]
                m_new = jnp.maximum(f_last[h][c] + m_prev, m_chunk[h][c])
                a = jnp.exp(f_last[h][c] + m_prev - m_new)
                b = jnp.exp(m_chunk[h][c] - m_new)
                c_st[h].append(a * c_st[h][c] + b * kv[h][c])
                n_st[h].append(a * n_st[h][c] + b * nk[h][c])
                m_st[h].append(m_new)
        for h in heads:
            c_ref[h], n_ref[h], m_ref[h] = c_st[h][n_chunks], n_st[h][n_chunks], m_st[h][n_chunks]
        side.step()
        inter = [[_dot_nt(qb[h][cs], c_st[h][c].astype(BF16)) for c, cs in enumerate(csl)] for h in heads]
        qn = [[jnp.sum(q[h][cs] * n_st[h][c], axis=-1, keepdims=True) for c, cs in enumerate(csl)]
              for h in heads]
        side.step()
        gate = jax.nn.sigmoid(_dot(x_ref[sl, :], wg_ref[...]) + bg_ref[...])
        ys = []
        for h in heads:
            side.step()
            outs = []
            for c, cs in enumerate(csl):
                a_t = fc[h][cs] + m_st[h][c]
                m_t = jnp.maximum(a_t, m_intra[h][cs])
                w_inter = jnp.exp(a_t - m_t)
                w_intra = jnp.exp(m_intra[h][cs] - m_t)
                num = w_inter * inter[h][c] + w_intra * num_intra[h][cs]
                den = w_inter * qn[h][c] + w_intra * den_intra[h][cs]
                outs.append(num / jnp.maximum(jnp.abs(den), jnp.exp(-m_t)))
            hn = _head_norm(jnp.concatenate(outs, axis=0))
            og = z[pad:, M_O + h * ML_DH:M_O + (h + 1) * ML_DH]
            ys.append((hn * gn_ref[:, hsl[h]] * jax.nn.sigmoid(og)).astype(BF16))
        side.flush()
        y = jnp.concatenate(ys, axis=1)
        o_ref[sl, :] = (gate * _dot(y, wp_ref[...])).astype(o_ref.dtype)

    _tile_pairs(seq // rows, x_ref, xn_ref, win_ref, rows, tile, za_ref, zb_ref)


def _mlstm(xb, win, cw, cb, bi, bf, gn, wg, bg, wp, l):
    bsz, seq, _ = xb.shape
    xspec = pl.BlockSpec((None, seq, D_MODEL), lambda b: (b, 0, 0))
    return pl.pallas_call(
        _mlstm_kernel,
        grid=(bsz,),
        in_specs=[xspec, _next_tile_spec(bsz), _layer_spec(l, D_MODEL, M_W), _layer_spec(l, ML_CONV, 2 * ML_W),
                  _layer_spec(l, 1, 2 * ML_W), _layer_spec(l, 1, LANES), _layer_spec(l, 1, LANES),
                  _layer_spec(l, 1, ML_W), _layer_spec(l, D_MODEL, D_MODEL), _layer_spec(l, 1, D_MODEL),
                  _layer_spec(l, ML_W, D_MODEL)],
        out_specs=xspec,
        out_shape=jax.ShapeDtypeStruct((bsz, seq, D_MODEL), BF16),
        scratch_shapes=[pltpu.VMEM((ML_HEADS, ML_DH, ML_DH), F32),
                        pltpu.VMEM((ML_HEADS, 1, ML_DH), F32),
                        pltpu.VMEM((ML_HEADS, 1, 1), F32),
                        pltpu.VMEM((SUBLANES + MIX_ROWS, M_W), F32),
                        pltpu.VMEM((SUBLANES + MIX_ROWS, M_W), F32)],
        compiler_params=_params("arbitrary"),
        name="mlstm",
    )(xb, xb, win, cw, cb, bi, bf, gn, wg, bg, wp)


def _merge_ln_kernel(x_ref, xb_ref, gg_ref, gm_ref, yd_ref, wg_ref, bg_ref, wpd_ref, wo_ref, g_ref, b_ref,
                     o_ref):
    gate_d = jax.nn.sigmoid(_dot(xb_ref[...], wg_ref[...]) + bg_ref[...])
    merged = gg_ref[...].astype(F32) + gm_ref[...].astype(F32) + gate_d * _dot(yd_ref[...], wpd_ref[...])
    mix = _dot(merged.astype(BF16), wo_ref[...])
    o_ref[...] = _ln(DN_ALPHA * x_ref[...] + mix, g_ref[...], b_ref[...])


def _merge_ln(x2d, xb2d, gg, gm, yd, wg, bg, wpd, wo, g, b, l):
    m = x2d.shape[0]

    def row(width):
        return pl.BlockSpec((TM, width), lambda i: (i, 0))

    return pl.pallas_call(
        _merge_ln_kernel,
        grid=(m // TM,),
        in_specs=[row(D_MODEL), row(D_MODEL), row(D_MODEL), row(D_MODEL), row(DIL_W),
                  _layer_spec(l, D_MODEL, D_MODEL), _layer_spec(l, 1, D_MODEL),
                  _layer_spec(l, DIL_W, D_MODEL), _layer_spec(l, D_MODEL, D_MODEL),
                  _layer_spec(l, 1, D_MODEL), _layer_spec(l, 1, D_MODEL)],
        out_specs=row(D_MODEL),
        out_shape=jax.ShapeDtypeStruct((m, D_MODEL), F32),
        compiler_params=_params("parallel"),
        name="merge_ln",
    )(x2d, xb2d, gg, gm, yd, wg, bg, wpd, wo, g, b)


def _stage_w_in(w_in):
    w_in = w_in.astype(BF16)

    def cols(idx):
        return w_in[:, :, IN_OFFS[idx]:IN_OFFS[idx + 1]]

    depth = w_in.shape[0]
    small = jnp.concatenate(
        [cols(4), cols(11), cols(12),
         jnp.zeros((depth, D_MODEL, LANES - GLA_RANK - 2 * ML_HEADS), w_in.dtype)], axis=-1)
    w_gla = jnp.concatenate([cols(0), cols(1), cols(2), cols(3), small], axis=-1)
    w_dil = jnp.concatenate([cols(5), cols(6), cols(7)], axis=-1)
    w_ml = jnp.concatenate([cols(8), cols(9), cols(10), cols(13), small], axis=-1)
    gates = cols(14)
    d = D_MODEL
    return w_gla, w_dil, w_ml, gates[:, :, 0:d], gates[:, :, d:2 * d], gates[:, :, 2 * d:3 * d]


def _lane_vec(vals, lane0):
    depth, n = vals.shape
    out = jnp.zeros((depth, 1, LANES), F32)
    return out.at[:, 0, lane0:lane0 + n].set(vals.astype(F32))


def kernel(x, ffn1_w_gate, ffn1_w_up, ffn1_w_down, ln1_g, ln1_b, w_in, gla_w_a2, gla_b_a, gla_norm_g,
           ml_conv_w, ml_conv_b, ml_b_i, ml_b_f, ml_norm_g, w_proj_gla, w_proj_dil, w_proj_ml, b_gate,
           w_out, ln2_g, ln2_b, ffn2_w_gate, ffn2_w_up, ffn2_w_down, ln3_g, ln3_b):
    bsz, seq, d = x.shape
    depth = w_in.shape[0]
    assert d == D_MODEL and (bsz * seq) % TM == 0 and seq % (16 * DIL_BLOCK) == 0

    bf = lambda w: w.astype(BF16)
    row = lambda v: v.astype(F32)[:, None, :]
    f1g, f1u, f1d = bf(ffn1_w_gate), bf(ffn1_w_up), bf(ffn1_w_down)
    f2g, f2u, f2d = bf(ffn2_w_gate), bf(ffn2_w_up), bf(ffn2_w_down)
    w_gla, w_dil, w_ml, wg_gla, wg_dil, wg_ml = _stage_w_in(w_in)
    bg = b_gate.astype(F32)
    bg_gla, bg_dil, bg_ml = row(bg[:, 0:d]), row(bg[:, d:2 * d]), row(bg[:, 2 * d:3 * d])
    wa = jnp.zeros((depth, LANES, GLA_QK_W), BF16).at[:, SM_A:SM_A + GLA_RANK, :].set(bf(gla_w_a2))
    bi_vec, bf_vec = _lane_vec(ml_b_i, SM_I), _lane_vec(ml_b_f, SM_F)
    wpg, wpd, wpm, wo = bf(w_proj_gla), bf(w_proj_dil), bf(w_proj_ml), bf(w_out)
    bias = _dil_bias_table()

    h = x.reshape(bsz * seq, d)
    for l in range(depth):
        h, hb = _ffn_ln(h, f1g, f1u, f1d, row(ln1_g), row(ln1_b), l, True)
        hb3 = hb.reshape(bsz, seq, d)
        g_gla = _gla(hb3, w_gla, wa, row(gla_b_a), row(gla_norm_g), wg_gla, bg_gla, wpg, l)
        z_dil = _dil_proj(hb, w_dil, l).reshape(bsz, seq, 3 * DIL_W)
        y_dil = _dil(z_dil, bias)
        g_ml = _mlstm(hb3, w_ml, ml_conv_w.astype(F32), row(ml_conv_b), bi_vec, bf_vec, row(ml_norm_g),
                      wg_ml, bg_ml, wpm, l)
        flat = lambda y: y.reshape(bsz * seq, y.shape[-1])
        h = _merge_ln(h, hb, flat(g_gla), flat(g_ml), flat(y_dil), wg_dil, bg_dil, wpd, wo,
                      row(ln2_g), row(ln2_b), l)
        (h,) = _ffn_ln(h, f2g, f2u, f2d, row(ln3_g), row(ln3_b), l, False)
    return h.reshape(bsz, seq, d)
```
